```python
import math
import jax
import jax.numpy as jnp
from jax import lax
import numpy as np

D_MODEL = 2048
BATCH = 4
SEQ = 2048
DEPTH = 4
DEC_BATCH = 2
DEC_SEQ = 8192
PAST_LEN = 128

N_MIXERS = 2
N_ATTN_LAYERS = (DEPTH + 1) // 2
N_SSD_LAYERS = DEPTH // 2

DILATED_CONFIGS = ((128, 1), (512, 4), (2048, 16))
N_GROUPS = len(DILATED_CONFIGS)
ATTN_HEADS = 16
HEAD_DIM = D_MODEL // ATTN_HEADS
QKV_DIM = N_GROUPS * 3 * ATTN_HEADS * HEAD_DIM

EXPAND = 2
D_INNER = EXPAND * D_MODEL
SSD_HEADDIM = 64
SSD_HEADS = D_INNER // SSD_HEADDIM
SSD_GROUPS = 8
D_STATE = 128
D_CONV = 5
CHUNK = 128
CONV_DIM = D_INNER + 2 * SSD_GROUPS * D_STATE
IN_PROJ_DIM = D_INNER + CONV_DIM + 2 * SSD_HEADS

D_FF = 4 * D_MODEL

DEEPNORM_ALPHA = (2.0 * DEPTH) ** 0.25
DEEPNORM_BETA = (8.0 * DEPTH) ** -0.25
LN_EPS = 1e-5
NEG_BIG = -1e30

kernel_name = "hybrid_dilated_attn_ssd_encoder"


def layer_norm(x, g, b):
    xf = x.astype(jnp.float32)
    mu = jnp.mean(xf, axis=-1, keepdims=True)
    var = jnp.mean(jnp.square(xf - mu), axis=-1, keepdims=True)
    return ((xf - mu) * lax.rsqrt(var + LN_EPS) * g + b).astype(x.dtype)


def alibi_slopes(n):
    return 2.0 ** (-8.0 * jnp.arange(1, n + 1, dtype=jnp.float32) / n)


def band_attention(q, k, v, dil, half, slopes):
    n, l, h, hd = q.shape
    nb = -(-l // half)
    lp = nb * half
    qb = jnp.pad(q, ((0, 0), (0, lp - l), (0, 0), (0, 0))).reshape(n, nb, half, h, hd)
    kvpad = ((0, 0), (half, lp - l + half), (0, 0), (0, 0))
    kb = jnp.pad(k, kvpad).reshape(n, nb + 2, half, h, hd)
    vb = jnp.pad(v, kvpad).reshape(n, nb + 2, half, h, hd)
    kw = jnp.concatenate([kb[:, :-2], kb[:, 1:-1], kb[:, 2:]], axis=2)
    vw = jnp.concatenate([vb[:, :-2], vb[:, 1:-1], vb[:, 2:]], axis=2)
    scores = jnp.einsum("nbqhd,nbkhd->nbhqk", qb, kw).astype(jnp.float32) * (hd ** -0.5)
    qpos = jnp.arange(nb)[:, None] * half + jnp.arange(half)[None, :]
    kpos = jnp.arange(nb)[:, None] * half - half + jnp.arange(3 * half)[None, :]
    rel = kpos[:, None, :] - qpos[:, :, None]
    valid = (jnp.abs(rel) <= half) & (kpos[:, None, :] >= 0) & (kpos[:, None, :] < l)
    dist = jnp.abs(rel).astype(jnp.float32) * dil
    bias = -slopes[None, :, None, None] * dist[:, None]
    scores = jnp.where(valid[:, None], scores + bias, NEG_BIG)
    m = jnp.max(scores, axis=-1, keepdims=True)
    p = jnp.exp(scores - m)
    den = jnp.sum(p, axis=-1, keepdims=True)
    out = jnp.einsum("nbhqk,nbkhd->nbqhd", (p / den).astype(v.dtype), vw)
    lse = (m + jnp.log(den))[..., 0]
    out = out.reshape(n, lp, h, hd)[:, :l]
    lse = lse.transpose(0, 1, 3, 2).reshape(n, lp, h)[:, :l]
    return out, lse


def dilated_group(q, k, v, window, dil, slopes):
    b, s, h, hd = q.shape
    l = s // dil
    half = window // (2 * dil)

    def to_sub(t):
        return t.reshape(b, l, dil, h, hd).transpose(0, 2, 1, 3, 4).reshape(b * dil, l, h, hd)

    out, lse = band_attention(to_sub(q), to_sub(k), to_sub(v), dil, half, slopes)
    out = out.reshape(b, dil, l, h, hd).transpose(0, 2, 1, 3, 4).reshape(b, s, h, hd)
    lse = lse.reshape(b, dil, l, h).transpose(0, 2, 1, 3).reshape(b, s, h)
    return out, lse


def attention_mixer(x, w_qkv, w_o):
    b, s, _ = x.shape
    qkv = (x @ w_qkv).reshape(b, s, N_GROUPS, 3, ATTN_HEADS, HEAD_DIM)
    slopes = alibi_slopes(N_GROUPS * ATTN_HEADS).reshape(N_GROUPS, ATTN_HEADS)
    outs, lses = [], []
    for g, (window, dil) in enumerate(DILATED_CONFIGS):
        o, lse = dilated_group(qkv[:, :, g, 0], qkv[:, :, g, 1], qkv[:, :, g, 2], window, dil, slopes[g])
        outs.append(o)
        lses.append(lse)
    w = jax.nn.softmax(jnp.stack(lses, axis=0), axis=0)
    o = jnp.einsum("gbsh,gbshd->bshd", w.astype(x.dtype), jnp.stack(outs, axis=0))
    return o.reshape(b, s, ATTN_HEADS * HEAD_DIM) @ w_o


def centred_depthwise_conv(u, w, bias):
    c = u.shape[-1]
    pad = D_CONV // 2
    out = lax.conv_general_dilated(u, w[:, None, :].astype(u.dtype), window_strides=(1,),
                                   padding=[(pad, pad)], dimension_numbers=("NWC", "WIO", "NWC"),
                                   feature_group_count=c)
    return out + bias


def ssd_scan(x, dt, a, bm, cm):
    b, s, h, p = x.shape
    g, n = bm.shape[2], bm.shape[3]
    e = h // g
    c = s // CHUNK
    l = CHUNK
    x = x.astype(jnp.float32)
    dt = dt.astype(jnp.float32)
    xc = (x * dt[..., None]).reshape(b, c, l, g, e, p)
    la = (dt * a.astype(jnp.float32)).reshape(b, c, l, g, e)
    a_cs = jnp.cumsum(la, axis=2)
    bc = bm.astype(jnp.float32).reshape(b, c, l, g, n)
    cc = cm.astype(jnp.float32).reshape(b, c, l, g, n)
    seg = a_cs[:, :, :, None] - a_cs[:, :, None, :]
    tri = jnp.tril(jnp.ones((l, l), dtype=bool))[None, None, :, :, None, None]
    lmat = jnp.exp(jnp.where(tri, seg, -jnp.inf))
    cb = jnp.einsum("bctgn,bcsgn->bctsg", cc, bc)
    y_diag = jnp.einsum("bctsg,bctsge,bcsgep->bctgep", cb, lmat, xc)
    decay_to_end = jnp.exp(a_cs[:, :, -1:] - a_cs)
    chunk_states = jnp.einsum("bclgn,bclge,bclgep->bcgepn", bc, decay_to_end, xc)
    chunk_decay = jnp.exp(a_cs[:, :, -1])

    def step(state, inp):
        cs, cd = inp
        return cd[..., None, None] * state + cs, state

    init = jnp.zeros((b, g, e, p, n), dtype=jnp.float32)
    _, prev = lax.scan(step, init, (jnp.moveaxis(chunk_states, 1, 0), jnp.moveaxis(chunk_decay, 1, 0)))
    prev = jnp.moveaxis(prev, 0, 1)
    y_off = jnp.einsum("bclgn,bcgepn,bclge->bclgep", cc, prev, jnp.exp(a_cs))
    return (y_diag + y_off).reshape(b, s, h, p)


def gated_group_rmsnorm(y, z, w):
    b, s, d = y.shape
    u = (y.astype(jnp.float32) * jax.nn.silu(z.astype(jnp.float32))).reshape(b, s, SSD_GROUPS, d // SSD_GROUPS)
    u = u * lax.rsqrt(jnp.mean(jnp.square(u), axis=-1, keepdims=True) + LN_EPS)
    return (u.reshape(b, s, d) * w).astype(z.dtype)


def ssd_mixer(x, w_in, conv_w, conv_b, dt_bias, a_log, d_skip, norm_w, w_out):
    b, s, _ = x.shape
    zxbcdt = x @ w_in
    z = zxbcdt[..., :D_INNER]
    xbc = zxbcdt[..., D_INNER:D_INNER + CONV_DIM]
    dt_raw = zxbcdt[..., D_INNER + CONV_DIM:]
    xbc = jax.nn.silu(centred_depthwise_conv(xbc, conv_w, conv_b))
    xs = xbc[..., :D_INNER].reshape(b, s, SSD_HEADS, SSD_HEADDIM)
    bm = xbc[..., D_INNER:D_INNER + SSD_GROUPS * D_STATE].reshape(b, s, SSD_GROUPS, D_STATE)
    cm = xbc[..., D_INNER + SSD_GROUPS * D_STATE:].reshape(b, s, SSD_GROUPS, D_STATE)
    dt = jax.nn.softplus(dt_raw.astype(jnp.float32).reshape(b, s, 2, SSD_HEADS) + dt_bias.astype(jnp.float32))
    a = -jnp.exp(a_log.astype(jnp.float32))
    y_fwd = ssd_scan(xs, dt[:, :, 0], a[0], bm, cm)
    y_bwd = ssd_scan(xs[:, ::-1], dt[:, ::-1, 1], a[1], bm[:, ::-1], cm[:, ::-1])[:, ::-1]
    y = y_fwd + y_bwd + d_skip.astype(jnp.float32)[:, None] * xs.astype(jnp.float32)
    y = gated_group_rmsnorm(y.reshape(b, s, D_INNER), z, norm_w)
    return y @ w_out


def squared_relu_mlp(x, w1, w2):
    h = jax.nn.relu(x @ w1)
    return (h * h) @ w2


def run_trunk(x, attn_w_qkv, attn_w_o, ssd_w_in, ssd_conv_w, ssd_conv_b, ssd_dt_bias, ssd_a_log,
              ssd_d, ssd_norm_w, ssd_w_out, mlp_w1, mlp_w2, ln_g, ln_b):
    for i in range(DEPTH):
        j = i // N_MIXERS
        if i % N_MIXERS == 0:
            mix = attention_mixer(x, attn_w_qkv[j], attn_w_o[j])
        else:
            mix = ssd_mixer(x, ssd_w_in[j], ssd_conv_w[j], ssd_conv_b[j], ssd_dt_bias[j], ssd_a_log[j],
                            ssd_d[j], ssd_norm_w[j], ssd_w_out[j])
        x = layer_norm(DEEPNORM_ALPHA * x + mix, ln_g[i, 0], ln_b[i, 0])
        x = layer_norm(DEEPNORM_ALPHA * x + squared_relu_mlp(x, mlp_w1[i], mlp_w2[i]), ln_g[i, 1], ln_b[i, 1])
    return x


def setup_inputs(seed: int = 0) -> dict:
    key = jax.random.key(seed)
    ks = jax.random.split(key, 20)
    f32 = jnp.float32
    nrm = lambda k, shape, scale: jax.random.normal(k, shape, dtype=f32) * scale
    dt0 = jnp.exp(jax.random.uniform(ks[7], (N_SSD_LAYERS, 2, SSD_HEADS), dtype=f32,
                                     minval=math.log(1e-3), maxval=math.log(1e-1)))
    return {
        "x_prompt": nrm(ks[0], (BATCH, SEQ, D_MODEL), 1.0),
        "x_sample": nrm(ks[1], (DEC_BATCH, DEC_SEQ, D_MODEL), 1.0),
        "attn_w_qkv": nrm(ks[2], (N_ATTN_LAYERS, D_MODEL, QKV_DIM), D_MODEL ** -0.5),
        "attn_w_o": nrm(ks[3], (N_ATTN_LAYERS, ATTN_HEADS * HEAD_DIM, D_MODEL), DEEPNORM_BETA * (ATTN_HEADS * HEAD_DIM) ** -0.5),
        "ssd_w_in": nrm(ks[4], (N_SSD_LAYERS, D_MODEL, IN_PROJ_DIM), D_MODEL ** -0.5),
        "ssd_conv_w": nrm(ks[5], (N_SSD_LAYERS, D_CONV, CONV_DIM), D_CONV ** -0.5),
        "ssd_conv_b": nrm(ks[6], (N_SSD_LAYERS, CONV_DIM), 0.01),
        "ssd_dt_bias": dt0 + jnp.log(-jnp.expm1(-dt0)),
        "ssd_a_log": jnp.log(jax.random.uniform(ks[8], (N_SSD_LAYERS, 2, SSD_HEADS), dtype=f32, minval=1.0, maxval=16.0)),
        "ssd_d": 1.0 + nrm(ks[9], (N_SSD_LAYERS, SSD_HEADS), 0.1),
        "ssd_norm_w": 1.0 + nrm(ks[10], (N_SSD_LAYERS, D_INNER), 0.02),
        "ssd_w_out": nrm(ks[11], (N_SSD_LAYERS, D_INNER, D_MODEL), DEEPNORM_BETA * D_INNER ** -0.5),
        "mlp_w1": nrm(ks[12], (DEPTH, D_MODEL, D_FF), D_MODEL ** -0.5),
        "mlp_w2": nrm(ks[13], (DEPTH, D_FF, D_MODEL), DEEPNORM_BETA * D_FF ** -0.5),
        "ln_g": 1.0 + nrm(ks[14], (DEPTH, 2, D_MODEL), 0.02),
        "ln_b": nrm(ks[15], (DEPTH, 2, D_MODEL), 0.02),
    }


def reference(x_prompt, x_sample, attn_w_qkv, attn_w_o, ssd_w_in, ssd_conv_w, ssd_conv_b, ssd_dt_bias,
              ssd_a_log, ssd_d, ssd_norm_w, ssd_w_out, mlp_w1, mlp_w2, ln_g, ln_b):
    y_prompt = run_trunk(x_prompt, attn_w_qkv, attn_w_o, ssd_w_in, ssd_conv_w, ssd_conv_b, ssd_dt_bias,
                         ssd_a_log, ssd_d, ssd_norm_w, ssd_w_out, mlp_w1, mlp_w2, ln_g, ln_b)
    y_sample = run_trunk(x_sample, attn_w_qkv, attn_w_o, ssd_w_in, ssd_conv_w, ssd_conv_b, ssd_dt_bias,
                         ssd_a_log, ssd_d, ssd_norm_w, ssd_w_out, mlp_w1, mlp_w2, ln_g, ln_b)
    return (y_prompt, y_sample)
```

```python
import functools
import math

import numpy as np
import jax
import jax.numpy as jnp
from jax import lax
from jax.experimental import pallas as pl
from jax.experimental.pallas import tpu as pltpu

F32 = jnp.float32
BF16 = jnp.bfloat16

DILATED_CONFIGS = ((128, 1), (512, 4), (2048, 16))
ATTN_HEADS = 16
HEAD_DIM = 128
SSD_HEADDIM = 64
SSD_GROUPS = 8
D_STATE = 128
D_CONV = 5
CHUNK = 128
LN_EPS = 1e-5
NEG_BIG = -1e30

V7X_VMEM_LIMIT_BYTES = 56 * 1024 * 1024
LANES = 128
SUBLANES = 8


def _cparams(*sem):
    return pltpu.CompilerParams(dimension_semantics=sem, vmem_limit_bytes=V7X_VMEM_LIMIT_BYTES)


def _layer_norm_rows(y, g, b):
    mu = jnp.mean(y, axis=-1, keepdims=True)
    yc = y - mu
    var = jnp.mean(yc * yc, axis=-1, keepdims=True)
    return yc * lax.rsqrt(var + LN_EPS) * g + b


def _seq_bounds(row, layout):
    rows_p, seq_p, seq_s = layout
    in_p = row < rows_p
    lo_p = (row // seq_p) * seq_p
    lo_s = rows_p + ((row - rows_p) // seq_s) * seq_s
    lo = jnp.where(in_p, lo_p, lo_s)
    return lo, lo + jnp.where(in_p, seq_p, seq_s)


def _mm_kernel(x_ref, w_ref, o_ref, *, act):
    acc = jnp.dot(x_ref[...], w_ref[...], preferred_element_type=F32)
    if act == "relu2":
        acc = jnp.maximum(acc, 0.0)
        acc = acc * acc
    o_ref[...] = acc.astype(o_ref.dtype)


def _matmul(x, w, out_dtype, *, bm, bn, act=None, name):
    t, k = x.shape
    n = w.shape[1]
    bn = min(bn, n)
    return pl.pallas_call(
        functools.partial(_mm_kernel, act=act),
        grid=(t // bm, n // bn),
        in_specs=[pl.BlockSpec((bm, k), lambda i, j: (i, 0)),
                  pl.BlockSpec((k, bn), lambda i, j: (0, j))],
        out_specs=pl.BlockSpec((bm, bn), lambda i, j: (i, j)),
        out_shape=jax.ShapeDtypeStruct((t, n), out_dtype),
        compiler_params=_cparams("parallel", "parallel"),
        name=name,
    )(x, w)


def _mm_ln_kernel(l_ref, w_ref, r_ref, g_ref, b_ref, o_ref, ob_ref, acc_ref, *, nk, alpha):
    k = pl.program_id(1)
    part = jnp.dot(l_ref[...], w_ref[...], preferred_element_type=F32)

    @pl.when(k == 0)
    def _():
        acc_ref[...] = part

    @pl.when(k > 0)
    def _():
        acc_ref[...] += part

    @pl.when(k == nk - 1)
    def _():
        y = _layer_norm_rows(alpha * r_ref[...] + acc_ref[...], g_ref[...], b_ref[...])
        o_ref[...] = y
        ob_ref[...] = y.astype(BF16)


def _matmul_res_ln(lhs, w, res, g, b, *, alpha, bm, bk, name):
    t, k = lhs.shape
    n = w.shape[1]
    nk = k // bk
    return pl.pallas_call(
        functools.partial(_mm_ln_kernel, nk=nk, alpha=alpha),
        grid=(t // bm, nk),
        in_specs=[pl.BlockSpec((bm, bk), lambda i, kk: (i, kk)),
                  pl.BlockSpec((bk, n), lambda i, kk: (kk, 0)),
                  pl.BlockSpec((bm, n), lambda i, kk: (i, 0)),
                  pl.BlockSpec((1, n), lambda i, kk: (0, 0)),
                  pl.BlockSpec((1, n), lambda i, kk: (0, 0))],
        out_specs=[pl.BlockSpec((bm, n), lambda i, kk: (i, 0)),
                   pl.BlockSpec((bm, n), lambda i, kk: (i, 0))],
        out_shape=[jax.ShapeDtypeStruct((t, n), F32), jax.ShapeDtypeStruct((t, n), BF16)],
        scratch_shapes=[pltpu.VMEM((bm, n), F32)],
        compiler_params=_cparams("parallel", "arbitrary"),
        name=name,
    )(lhs, w, res, g, b)


def _mlp_kernel(x_ref, xb_ref, w1_ref, w2_ref, g_ref, b_ref, o_ref, ob_ref, acc_ref, *, nf, alpha):
    f = pl.program_id(1)
    h = jnp.dot(xb_ref[...], w1_ref[...], preferred_element_type=F32)
    h = jnp.maximum(h, 0.0)
    h = (h * h).astype(BF16)
    part = jnp.dot(h, w2_ref[...], preferred_element_type=F32)

    @pl.when(f == 0)
    def _():
        acc_ref[...] = part

    @pl.when(f > 0)
    def _():
        acc_ref[...] += part

    @pl.when(f == nf - 1)
    def _():
        y = _layer_norm_rows(alpha * x_ref[...] + acc_ref[...], g_ref[...], b_ref[...])
        o_ref[...] = y
        ob_ref[...] = y.astype(BF16)


def _mlp_res_ln(x, xb, w1, w2, g, b, *, alpha, bm, bf, name):
    t, d = x.shape
    dff = w1.shape[1]
    nf = dff // bf
    return pl.pallas_call(
        functools.partial(_mlp_kernel, nf=nf, alpha=alpha),
        grid=(t // bm, nf),
        in_specs=[pl.BlockSpec((bm, d), lambda i, f: (i, 0)),
                  pl.BlockSpec((bm, d), lambda i, f: (i, 0)),
                  pl.BlockSpec((d, bf), lambda i, f: (0, f)),
                  pl.BlockSpec((bf, d), lambda i, f: (f, 0)),
                  pl.BlockSpec((1, d), lambda i, f: (0, 0)),
                  pl.BlockSpec((1, d), lambda i, f: (0, 0))],
        out_specs=[pl.BlockSpec((bm, d), lambda i, f: (i, 0)),
                   pl.BlockSpec((bm, d), lambda i, f: (i, 0))],
        out_shape=[jax.ShapeDtypeStruct((t, d), F32), jax.ShapeDtypeStruct((t, d), BF16)],
        scratch_shapes=[pltpu.VMEM((bm, d), F32)],
        compiler_params=_cparams("parallel", "arbitrary"),
        name=name,
    )(x, xb, w1, w2, g, b)


def _attn_kernel(q_ref, kp_ref, kc_ref, kn_ref, vp_ref, vc_ref, vn_ref, o_ref, lse_ref,
                 k_scr, v_scr, *, tq, half, dil, slopes, layout):
    j = pl.program_id(1)
    row0 = j * tq
    lo, hi = _seq_bounds(row0, layout)
    wk = tq + 2 * half
    k_scr[0:half, :] = kp_ref[...]
    k_scr[half:half + tq, :] = kc_ref[...]
    k_scr[half + tq:wk, :] = kn_ref[...]
    v_scr[0:half, :] = vp_ref[...]
    v_scr[half:half + tq, :] = vc_ref[...]
    v_scr[half + tq:wk, :] = vn_ref[...]

    qi = lax.broadcasted_iota(jnp.int32, (tq, wk), 0)
    ki = lax.broadcasted_iota(jnp.int32, (tq, wk), 1)
    rel = ki - half - qi
    kpos = row0 - half + ki
    valid = (jnp.abs(rel) <= half) & (kpos >= lo) & (kpos < hi)
    negdist = -(jnp.abs(rel) * dil).astype(F32)
    maskadd = jnp.where(valid, 0.0, NEG_BIG).astype(F32)
    scale = HEAD_DIM ** -0.5
    lane = lax.broadcasted_iota(jnp.int32, (tq, LANES), 1)
    lse_all = jnp.zeros((tq, LANES), F32)

    for h in range(ATTN_HEADS):
        cs = slice(h * HEAD_DIM, (h + 1) * HEAD_DIM)
        s = lax.dot_general(q_ref[:, cs], k_scr[:, cs], (((1,), (1,)), ((), ())),
                            preferred_element_type=F32)
        s = s * scale + (negdist * slopes[h] + maskadd)
        m = jnp.max(s, axis=-1, keepdims=True)
        p = jnp.exp(s - m)
        den = jnp.sum(p, axis=-1, keepdims=True)
        pn = (p * (1.0 / den)).astype(BF16)
        o_ref[:, cs] = jnp.dot(pn, v_scr[:, cs], preferred_element_type=F32)
        lse_all = jnp.where(lane == h, m + jnp.log(den), lse_all)
    lse_ref[...] = lse_all


def _band_attention(qkv, *, dil, window, slopes, layout, tq, name):
    t = qkv.shape[0]
    hd_all = ATTN_HEADS * HEAD_DIM
    half = window // (2 * dil)
    rows = t // dil
    nt = rows // tq
    per = tq // half
    nhalf = t // half
    lay = tuple(v // dil for v in layout)

    def cur(col):
        return pl.BlockSpec((tq, hd_all), lambda r, j: (r * nt + j, col))

    def prev(col):
        return pl.BlockSpec((half, hd_all), lambda r, j: (jnp.maximum((r * nt + j) * per - 1, 0), col))

    def nxt(col):
        return pl.BlockSpec((half, hd_all), lambda r, j: (jnp.minimum((r * nt + j + 1) * per, nhalf - 1), col))

    return pl.pallas_call(
        functools.partial(_attn_kernel, tq=tq, half=half, dil=dil, slopes=slopes, layout=lay),
        grid=(dil, nt),
        in_specs=[cur(0), prev(1), cur(1), nxt(1), prev(2), cur(2), nxt(2)],
        out_specs=[pl.BlockSpec((tq, hd_all), lambda r, j: (r * nt + j, 0)),
                   pl.BlockSpec((tq, LANES), lambda r, j: (r * nt + j, 0))],
        out_shape=[jax.ShapeDtypeStruct((t, hd_all), F32), jax.ShapeDtypeStruct((t, LANES), F32)],
        scratch_shapes=[pltpu.VMEM((tq + 2 * half, hd_all), BF16),
                        pltpu.VMEM((tq + 2 * half, hd_all), BF16)],
        compiler_params=_cparams("parallel", "parallel"),
        name=name,
    )(qkv, qkv, qkv, qkv, qkv, qkv, qkv)


def _combine_ln_kernel(o0_ref, o1_ref, o2_ref, l0_ref, l1_ref, l2_ref, w_ref, r_ref, g_ref, b_ref,
                       o_ref, ob_ref, mix_ref, *, alpha):
    l0, l1, l2 = l0_ref[...], l1_ref[...], l2_ref[...]
    m = jnp.maximum(jnp.maximum(l0, l1), l2)
    e0, e1, e2 = jnp.exp(l0 - m), jnp.exp(l1 - m), jnp.exp(l2 - m)
    inv = 1.0 / (e0 + e1 + e2)
    w0, w1, w2 = e0 * inv, e1 * inv, e2 * inv
    for h in range(ATTN_HEADS):
        cs = slice(h * HEAD_DIM, (h + 1) * HEAD_DIM)
        mix = (w0[:, h:h + 1] * o0_ref[:, cs] + w1[:, h:h + 1] * o1_ref[:, cs]
               + w2[:, h:h + 1] * o2_ref[:, cs])
        mix_ref[:, cs] = mix.astype(BF16)
    acc = jnp.dot(mix_ref[...], w_ref[...], preferred_element_type=F32)
    y = _layer_norm_rows(alpha * r_ref[...] + acc, g_ref[...], b_ref[...])
    o_ref[...] = y
    ob_ref[...] = y.astype(BF16)


def _combine_proj_ln(outs, lses, w, res, g, b, *, alpha, bm, name):
    t, d = res.shape
    k = w.shape[0]
    row = lambda i: (i, 0)
    const = lambda i: (0, 0)
    return pl.pallas_call(
        functools.partial(_combine_ln_kernel, alpha=alpha),
        grid=(t // bm,),
        in_specs=[pl.BlockSpec((bm, k), row)] * 3 + [pl.BlockSpec((bm, LANES), row)] * 3
                 + [pl.BlockSpec((k, d), const), pl.BlockSpec((bm, d), row),
                    pl.BlockSpec((1, d), const), pl.BlockSpec((1, d), const)],
        out_specs=[pl.BlockSpec((bm, d), row), pl.BlockSpec((bm, d), row)],
        out_shape=[jax.ShapeDtypeStruct((t, d), F32), jax.ShapeDtypeStruct((t, d), BF16)],
        scratch_shapes=[pltpu.VMEM((bm, k), BF16)],
        compiler_params=_cparams("parallel"),
        name=name,
    )(*outs, *lses, w, res, g, b)


def _conv_kernel(p_ref, c_ref, n_ref, w_ref, b_ref, o_ref, ext_ref, *, tr, layout):
    i = pl.program_id(0)
    row0 = i * tr
    lo, hi = _seq_bounds(row0, layout)
    halo = SUBLANES
    ext_ref[0:halo, :] = jnp.where(row0 == lo, 0.0, p_ref[...])
    ext_ref[halo:halo + tr, :] = c_ref[...]
    ext_ref[halo + tr:2 * halo + tr, :] = jnp.where(row0 + tr == hi, 0.0, n_ref[...])
    acc = b_ref[...] + w_ref[0:1, :] * ext_ref[pl.ds(halo - 2, tr), :]
    for kk in range(1, D_CONV):
        acc = acc + w_ref[kk:kk + 1, :] * ext_ref[pl.ds(halo - 2 + kk, tr), :]
    o_ref[...] = (acc * (1.0 / (1.0 + jnp.exp(-acc)))).astype(o_ref.dtype)


def _conv_silu(u, w, b, *, col0, ncols, out_dtype, layout, tr, bc, name):
    t = u.shape[0]
    c0 = col0 // bc
    per = tr // SUBLANES
    n8 = t // SUBLANES
    return pl.pallas_call(
        functools.partial(_conv_kernel, tr=tr, layout=layout),
        grid=(t // tr, ncols // bc),
        in_specs=[pl.BlockSpec((SUBLANES, bc), lambda i, j: (jnp.maximum(i * per - 1, 0), c0 + j)),
                  pl.BlockSpec((tr, bc), lambda i, j: (i, c0 + j)),
                  pl.BlockSpec((SUBLANES, bc), lambda i, j: (jnp.minimum((i + 1) * per, n8 - 1), c0 + j)),
                  pl.BlockSpec((D_CONV, bc), lambda i, j: (0, c0 + j)),
                  pl.BlockSpec((1, bc), lambda i, j: (0, c0 + j))],
        out_specs=pl.BlockSpec((tr, bc), lambda i, j: (i, j)),
        out_shape=jax.ShapeDtypeStruct((t, ncols), out_dtype),
        scratch_shapes=[pltpu.VMEM((tr + 2 * SUBLANES, bc), F32)],
        compiler_params=_cparams("parallel", "parallel"),
        name=name,
    )(u, u, u, w, b)


def _split3(x):
    hi = x.astype(BF16)
    r1 = x - hi.astype(F32)
    mid = r1.astype(BF16)
    lo = (r1 - mid.astype(F32)).astype(BF16)
    return hi, mid, lo


def _cumsum_rows(tri, la):
    rhs = jnp.concatenate(_split3(la), axis=1)
    out = jnp.dot(tri, rhs, preferred_element_type=F32)
    return out[:, 0:LANES] + out[:, LANES:2 * LANES] + out[:, 2 * LANES:3 * LANES]


def _expand_heads(q, e3_ref):
    lhs = jnp.concatenate(_split3(q), axis=1)
    return jnp.dot(lhs, e3_ref[...], preferred_element_type=F32)


def _softplus(x):
    return jnp.maximum(x, 0.0) + jnp.log(1.0 + jnp.exp(-jnp.abs(x)))


def _chunk_flags(c, layout_chunks):
    lo, hi = _seq_bounds(c, layout_chunks)
    return c == lo, c == hi - 1


def _ssd_bwd_kernel(x_ref, b_ref, dt_ref, dtb_ref, alog_ref, e3_ref, prev_ref, state_ref,
                    *, nchunks, layout_chunks, nheads):
    c = nchunks - 1 - pl.program_id(0)
    _, is_last = _chunk_flags(c, layout_chunks)

    @pl.when(is_last)
    def _():
        state_ref[...] = jnp.zeros_like(state_ref)

    prev_ref[...] = state_ref[...].astype(BF16)

    dt = _softplus(dt_ref[...] + dtb_ref[...])
    la = dt * (-jnp.exp(alog_ref[...]))
    ti = lax.broadcasted_iota(jnp.int32, (CHUNK, CHUNK), 0)
    si = lax.broadcasted_iota(jnp.int32, (CHUNK, CHUNK), 1)
    triu = jnp.where(si >= ti, 1.0, 0.0).astype(BF16)
    acs = _cumsum_rows(triu, la)[:, nheads:2 * nheads]
    dt_b = dt[:, nheads:2 * nheads]
    wgt = dt_b * jnp.exp(acs[0:1, :] - acs)
    xdec = (x_ref[...] * _expand_heads(wgt, e3_ref)).astype(BF16)
    cdec = _expand_heads(jnp.broadcast_to(jnp.exp(acs[0:1, :]), (SUBLANES, nheads)), e3_ref)[0:1, :]
    gw = (nheads // SSD_GROUPS) * SSD_HEADDIM
    for g in range(SSD_GROUPS):
        cs = slice(g * gw, (g + 1) * gw)
        bg = b_ref[:, g * D_STATE:(g + 1) * D_STATE]
        st = lax.dot_general(bg, xdec[:, cs], (((0,), (0,)), ((), ())), preferred_element_type=F32)
        state_ref[:, cs] = state_ref[:, cs] * cdec[:, cs] + st


def _ssd_fwd_kernel(x_ref, z_ref, b_ref, c_ref, dt_ref, prevb_ref, dtb_ref, alog_ref, dskip_ref,
                    nw_ref, e3_ref, o_ref, state_ref, y_ref, *, layout_chunks, nheads):
    c = pl.program_id(0)
    is_first, _ = _chunk_flags(c, layout_chunks)

    @pl.when(is_first)
    def _():
        state_ref[...] = jnp.zeros_like(state_ref)

    dt = _softplus(dt_ref[...] + dtb_ref[...])
    la = dt * (-jnp.exp(alog_ref[...]))
    ti = lax.broadcasted_iota(jnp.int32, (CHUNK, CHUNK), 0)
    si = lax.broadcasted_iota(jnp.int32, (CHUNK, CHUNK), 1)
    lower = ti >= si
    upper = si >= ti
    lanes2h = lax.broadcasted_iota(jnp.int32, (CHUNK, 2 * nheads), 1)
    acs = jnp.where(lanes2h < nheads,
                    _cumsum_rows(jnp.where(lower, 1.0, 0.0).astype(BF16), la),
                    _cumsum_rows(jnp.where(upper, 1.0, 0.0).astype(BF16), la))
    acs_t = acs.T
    dt_t = dt.T
    acs_f, acs_b = acs[:, 0:nheads], acs[:, nheads:2 * nheads]
    e_f = _expand_heads(jnp.exp(acs_f), e3_ref)
    e_b = _expand_heads(jnp.exp(acs_b), e3_ref)
    w_f = dt[:, 0:nheads] * jnp.exp(acs_f[CHUNK - 1:CHUNK, :] - acs_f)
    x = x_ref[...]
    xdec = (x * _expand_heads(w_f, e3_ref)).astype(BF16)
    cdec = e_f[CHUNK - 1:CHUNK, :]

    hpg = nheads // SSD_GROUPS
    gw = hpg * SSD_HEADDIM
    lane = lax.broadcasted_iota(jnp.int32, (CHUNK, LANES), 1)
    left = (lane < SSD_HEADDIM)
    for g in range(SSD_GROUPS):
        cs = slice(g * gw, (g + 1) * gw)
        bg = b_ref[:, g * D_STATE:(g + 1) * D_STATE]
        cg = c_ref[:, g * D_STATE:(g + 1) * D_STATE]
        cb = lax.dot_general(cg, bg, (((1,), (1,)), ((), ())), preferred_element_type=F32)
        y_off = (jnp.dot(cg, state_ref[:, cs].astype(BF16), preferred_element_type=F32) * e_f[:, cs]
                 + jnp.dot(cg, prevb_ref[:, cs], preferred_element_type=F32) * e_b[:, cs])
        for pr in range(hpg // 2):
            mats = []
            for h in (g * hpg + 2 * pr, g * hpg + 2 * pr + 1):
                hb = nheads + h
                seg_f = acs[:, h:h + 1] - acs_t[h:h + 1, :]
                seg_b = acs[:, hb:hb + 1] - acs_t[hb:hb + 1, :]
                dec = (jnp.exp(jnp.where(lower, seg_f, NEG_BIG)) * dt_t[h:h + 1, :]
                       + jnp.exp(jnp.where(upper, seg_b, NEG_BIG)) * dt_t[hb:hb + 1, :])
                mats.append((cb * dec).astype(BF16))
            ps = slice(g * gw + pr * LANES, g * gw + (pr + 1) * LANES)
            xp = x[:, ps]
            rhs = jnp.concatenate([jnp.where(left, xp, 0.0), jnp.where(left, 0.0, xp)], axis=0).astype(BF16)
            y_diag = jnp.dot(jnp.concatenate(mats, axis=1), rhs, preferred_element_type=F32)
            po = slice(pr * LANES, (pr + 1) * LANES)
            y_ref[:, ps] = y_diag + y_off[:, po] + dskip_ref[:, ps] * xp
        st = lax.dot_general(bg, xdec[:, cs], (((0,), (0,)), ((), ())), preferred_element_type=F32)
        state_ref[:, cs] = state_ref[:, cs] * cdec[:, cs] + st

    z = z_ref[...]
    u = y_ref[...] * (z * (1.0 / (1.0 + jnp.exp(-z))))
    for g in range(SSD_GROUPS):
        cs = slice(g * gw, (g + 1) * gw)
        ug = u[:, cs]
        ms = jnp.mean(ug * ug, axis=-1, keepdims=True)
        o_ref[:, cs] = (ug * lax.rsqrt(ms + LN_EPS) * nw_ref[:, cs]).astype(o_ref.dtype)


def _ssd_scan_norm(xs, z, bc, dt_raw, dt_bias, a_log, d_skip, norm_w, *, layout, name):
    t, d_inner = xs.shape
    nheads = d_inner // SSD_HEADDIM
    gn = SSD_GROUPS * D_STATE
    nchunks = t // CHUNK
    layout_chunks = tuple(v // CHUNK for v in layout)
    expand = np.repeat(np.eye(nheads, dtype=np.float32), SSD_HEADDIM, axis=1)
    e3 = jnp.asarray(np.concatenate([expand] * 3, axis=0), dtype=BF16)
    dtb = dt_bias.reshape(1, 2 * nheads).astype(F32)
    alog = a_log.reshape(1, 2 * nheads).astype(F32)
    dsk = jnp.repeat(d_skip.astype(F32), SSD_HEADDIM).reshape(1, d_inner)
    nw = norm_w.reshape(1, d_inner).astype(F32)
    const = lambda c: (0, 0)
    rev = lambda c: (nchunks - 1 - c, 0)
    fwd = lambda c: (c, 0)

    prev_b = pl.pallas_call(
        functools.partial(_ssd_bwd_kernel, nchunks=nchunks, layout_chunks=layout_chunks, nheads=nheads),
        grid=(nchunks,),
        in_specs=[pl.BlockSpec((CHUNK, d_inner), rev),
                  pl.BlockSpec((CHUNK, gn), rev),
                  pl.BlockSpec((CHUNK, 2 * nheads), rev),
                  pl.BlockSpec((1, 2 * nheads), const),
                  pl.BlockSpec((1, 2 * nheads), const),
                  pl.BlockSpec((3 * nheads, d_inner), const)],
        out_specs=pl.BlockSpec((D_STATE, d_inner), rev),
        out_shape=jax.ShapeDtypeStruct((nchunks * D_STATE, d_inner), BF16),
        scratch_shapes=[pltpu.VMEM((D_STATE, d_inner), F32)],
        compiler_params=_cparams("arbitrary"),
        name=name + "_bwd",
    )(xs, bc, dt_raw, dtb, alog, e3)

    return pl.pallas_call(
        functools.partial(_ssd_fwd_kernel, layout_chunks=layout_chunks, nheads=nheads),
        grid=(nchunks,),
        in_specs=[pl.BlockSpec((CHUNK, d_inner), fwd),
                  pl.BlockSpec((CHUNK, d_inner), fwd),
                  pl.BlockSpec((CHUNK, gn), lambda c: (c, 0)),
                  pl.BlockSpec((CHUNK, gn), lambda c: (c, 1)),
                  pl.BlockSpec((CHUNK, 2 * nheads), fwd),
                  pl.BlockSpec((D_STATE, d_inner), fwd),
                  pl.BlockSpec((1, 2 * nheads), const),
                  pl.BlockSpec((1, 2 * nheads), const),
                  pl.BlockSpec((1, d_inner), const),
                  pl.BlockSpec((1, d_inner), const),
                  pl.BlockSpec((3 * nheads, d_inner), const)],
        out_specs=pl.BlockSpec((CHUNK, d_inner), fwd),
        out_shape=jax.ShapeDtypeStruct((t, d_inner), BF16),
        scratch_shapes=[pltpu.VMEM((D_STATE, d_inner), F32), pltpu.VMEM((CHUNK, d_inner), F32)],
        compiler_params=_cparams("arbitrary"),
        name=name + "_fwd",
    )(xs, z, bc, bc, dt_raw, prev_b, dtb, alog, dsk, nw, e3)


def _to_sub_major(a, dil):
    t, c = a.shape
    return a.reshape(t // dil, dil, c).transpose(1, 0, 2).reshape(t, c)


def _from_sub_major(a, dil):
    t, c = a.shape
    return a.reshape(dil, t // dil, c).transpose(1, 0, 2).reshape(t, c)


def _alibi_slopes(n):
    return np.float32(2.0) ** (np.float32(-8.0) * np.arange(1, n + 1, dtype=np.float32) / np.float32(n))


def _attention_layer(x, xb, w_qkv, w_o, g, b, *, alpha, layout):
    ngroups = len(DILATED_CONFIGS)
    hd_all = ATTN_HEADS * HEAD_DIM
    slopes = _alibi_slopes(ngroups * ATTN_HEADS).reshape(ngroups, ATTN_HEADS)
    outs, lses = [], []
    for gi, (window, dil) in enumerate(DILATED_CONFIGS):
        wg = w_qkv[:, gi * 3 * hd_all:(gi + 1) * 3 * hd_all].astype(BF16)
        xg = xb if dil == 1 else _to_sub_major(xb, dil)
        qkv = _matmul(xg, wg, BF16, bm=1024, bn=1024, name=f"qkv_g{gi}")
        o, lse = _band_attention(qkv, dil=dil, window=window, slopes=tuple(float(s) for s in slopes[gi]),
                                 layout=layout, tq=128, name=f"attn_g{gi}")
        if dil > 1:
            o, lse = _from_sub_major(o, dil), _from_sub_major(lse, dil)
        outs.append(o)
        lses.append(lse)
    return _combine_proj_ln(outs, lses, w_o.astype(BF16), x, g, b, alpha=alpha, bm=256, name="attn_out")


def _ssd_layer(x, xb, w_in, conv_w, conv_b, dt_bias, a_log, d_skip, norm_w, w_out, g, b, *, alpha, layout):
    d_inner = w_out.shape[0]
    nheads = d_inner // SSD_HEADDIM
    gn = SSD_GROUPS * D_STATE
    conv_dim = d_inner + 2 * gn
    z = _matmul(xb, w_in[:, :d_inner].astype(BF16), F32, bm=1024, bn=1024, name="ssd_in_z")
    xbc = _matmul(xb, w_in[:, d_inner:d_inner + conv_dim].astype(BF16), F32, bm=1024, bn=1024, name="ssd_in_xbc")
    dt_raw = _matmul(xb, w_in[:, d_inner + conv_dim:].astype(BF16), F32, bm=1024, bn=2 * nheads, name="ssd_in_dt")
    cb2 = conv_b.reshape(1, conv_dim)
    xs = _conv_silu(xbc, conv_w, cb2, col0=0, ncols=d_inner, out_dtype=F32, layout=layout, tr=512, bc=1024,
                    name="ssd_conv_x")
    bc = _conv_silu(xbc, conv_w, cb2, col0=d_inner, ncols=2 * gn, out_dtype=BF16, layout=layout, tr=512, bc=1024,
                    name="ssd_conv_bc")
    y = _ssd_scan_norm(xs, z, bc, dt_raw, dt_bias, a_log, d_skip, norm_w, layout=layout, name="ssd_scan")
    return _matmul_res_ln(y, w_out.astype(BF16), x, g, b, alpha=alpha, bm=512, bk=1024, name="ssd_out")


def kernel(x_prompt, x_sample, attn_w_qkv, attn_w_o, ssd_w_in, ssd_conv_w, ssd_conv_b, ssd_dt_bias, ssd_a_log,
           ssd_d, ssd_norm_w, ssd_w_out, mlp_w1, mlp_w2, ln_g, ln_b):
    bp, sp, d = x_prompt.shape
    bs, ss, _ = x_sample.shape
    tp, ts = bp * sp, bs * ss
    depth = mlp_w1.shape[0]
    alpha = (2.0 * depth) ** 0.25
    layout = (tp, sp, ss)
    x = jnp.concatenate([x_prompt.reshape(tp, d), x_sample.reshape(ts, d)], axis=0)
    xb = x.astype(BF16)
    for i in range(depth):
        j = i // 2
        g0, b0 = ln_g[i, 0].reshape(1, d), ln_b[i, 0].reshape(1, d)
        g1, b1 = ln_g[i, 1].reshape(1, d), ln_b[i, 1].reshape(1, d)
        if i % 2 == 0:
            x, xb = _attention_layer(x, xb, attn_w_qkv[j], attn_w_o[j], g0, b0, alpha=alpha, layout=layout)
        else:
            x, xb = _ssd_layer(x, xb, ssd_w_in[j], ssd_conv_w[j], ssd_conv_b[j], ssd_dt_bias[j], ssd_a_log[j],
                               ssd_d[j], ssd_norm_w[j], ssd_w_out[j], g0, b0, alpha=alpha, layout=layout)
        x, xb = _mlp_res_ln(x, xb, mlp_w1[i].astype(BF16), mlp_w2[i].astype(BF16), g1, b1, alpha=alpha,
                            bm=512, bf=1024, name="mlp")
    return x[:tp].reshape(bp, sp, d), x[tp:].reshape(bs, ss, d)
```

```python
import functools
import math

import numpy as np
import jax
import jax.numpy as jnp
from jax import lax
from jax.experimental import pallas as pl
from jax.experimental.pallas import tpu as pltpu

F32 = jnp.float32
BF16 = jnp.bfloat16

DILATED_CONFIGS = ((128, 1), (512, 4), (2048, 16))
ATTN_HEADS = 16
HEAD_DIM = 128
SSD_HEADDIM = 64
SSD_GROUPS = 8
D_STATE = 128
D_CONV = 5
CHUNK = 128
LN_EPS = 1e-5
NEG_BIG = -1e30
LOG2E = 1.4426950408889634

V7X_VMEM_LIMIT_BYTES = 56 * 1024 * 1024
LANES = 128
SUBLANES = 8
BF16_ROWS = 16
CONV_ROWS = 128

TILES = dict(
    proj_bm=1024, proj_bn=1024,
    out_bm=1024, out_bk=512,
    out_rows=256,
    mix_bm=512,
    attn_tq=256,
    conv_tr=512, conv_bc=1024,
)


def _cparams(*sem):
    return pltpu.CompilerParams(dimension_semantics=sem, vmem_limit_bytes=V7X_VMEM_LIMIT_BYTES)


def _resident(block_shape, index_map):
    return pl.BlockSpec(block_shape, index_map, pipeline_mode=pl.Buffered(1))


def _layer_norm_rows(y, g, b):
    mu = jnp.mean(y, axis=-1, keepdims=True)
    yc = y - mu
    var = jnp.mean(yc * yc, axis=-1, keepdims=True)
    return yc * lax.rsqrt(var + LN_EPS) * g + b


def _seq_bounds(row, layout):
    rows_p, seq_p, seq_s = layout
    in_p = row < rows_p
    lo_p = (row // seq_p) * seq_p
    lo_s = rows_p + ((row - rows_p) // seq_s) * seq_s
    lo = jnp.where(in_p, lo_p, lo_s)
    return lo, lo + jnp.where(in_p, seq_p, seq_s)


def _split3(x):
    hi = x.astype(BF16)
    r1 = x - hi.astype(F32)
    mid = r1.astype(BF16)
    lo = (r1 - mid.astype(F32)).astype(BF16)
    return hi, mid, lo


def _mm_kernel(x_ref, w_ref, o_ref, wb_ref, *, act):
    @pl.when(pl.program_id(1) == 0)
    def _():
        wb_ref[...] = w_ref[...].astype(BF16)

    acc = jnp.dot(x_ref[...], wb_ref[...], preferred_element_type=F32)
    if act == "relu2":
        acc = jnp.maximum(acc, 0.0)
        acc = acc * acc
    o_ref[...] = acc.astype(o_ref.dtype)


def _matmul(x, w, layer, col0, n, out_dtype, *, act=None, name):
    t, k = x.shape
    bm = TILES["proj_bm"]
    bn = min(TILES["proj_bn"], n)
    c0 = col0 // bn
    return pl.pallas_call(
        functools.partial(_mm_kernel, act=act),
        grid=(n // bn, t // bm),
        in_specs=[pl.BlockSpec((bm, k), lambda j, i: (i, 0)),
                  pl.BlockSpec((None, k, bn), lambda j, i: (layer, 0, c0 + j))],
        out_specs=pl.BlockSpec((bm, bn), lambda j, i: (i, j)),
        out_shape=jax.ShapeDtypeStruct((t, n), out_dtype),
        scratch_shapes=[pltpu.VMEM((k, bn), BF16)],
        compiler_params=_cparams("parallel", "arbitrary"),
        name=name,
    )(x, w)


def _mm_ln_kernel(l_ref, w_ref, r_ref, g_ref, b_ref, o_ref, ob_ref, *, nk, alpha, rows):
    kk = pl.program_id(1)

    @pl.when(kk == 0)
    def _():
        o_ref[...] = jnp.zeros_like(o_ref)

    @pl.when(kk < nk - 1)
    def _():
        o_ref[...] += jnp.dot(l_ref[...], w_ref[...], preferred_element_type=F32)

    @pl.when(kk == nk - 1)
    def _():
        bm = l_ref.shape[0]
        for r0 in range(0, bm, rows):
            rs = slice(r0, r0 + rows)
            acc = o_ref[rs, :] + jnp.dot(l_ref[rs, :], w_ref[...], preferred_element_type=F32)
            y = _layer_norm_rows(alpha * r_ref[rs, :] + acc, g_ref[...], b_ref[...])
            o_ref[rs, :] = y
            ob_ref[rs, :] = y.astype(BF16)


def _matmul_res_ln(lhs, w, layer, res, g, b, *, alpha, name):
    t, k = lhs.shape
    n = w.shape[2]
    bm, bk = TILES["out_bm"], TILES["out_bk"]
    nk = k // bk
    return pl.pallas_call(
        functools.partial(_mm_ln_kernel, nk=nk, alpha=alpha, rows=TILES["out_rows"]),
        grid=(t // bm, nk),
        in_specs=[pl.BlockSpec((bm, bk), lambda i, kk: (i, kk)),
                  pl.BlockSpec((None, bk, n), lambda i, kk: (layer, kk, 0)),
                  pl.BlockSpec((bm, n), lambda i, kk: (i, 0)),
                  _resident((1, n), lambda i, kk: (0, 0)),
                  _resident((1, n), lambda i, kk: (0, 0))],
        out_specs=[pl.BlockSpec((bm, n), lambda i, kk: (i, 0)),
                   pl.BlockSpec((bm, n), lambda i, kk: (i, 0))],
        out_shape=[jax.ShapeDtypeStruct((t, n), F32), jax.ShapeDtypeStruct((t, n), BF16)],
        compiler_params=_cparams("parallel", "arbitrary"),
        name=name,
    )(lhs, w, res, g, b)


def _attn_kernel(q_ref, kp_ref, kc_ref, kn_ref, vp_ref, vc_ref, vn_ref, bias_ref, o_ref, lse_ref,
                 k_scr, v_scr, *, tq, half, layout):
    j = pl.program_id(1)
    row0 = j * tq
    lo, hi = _seq_bounds(row0, layout)
    wk = tq + 2 * half
    k_scr[0:half, :] = kp_ref[...]
    k_scr[half:half + tq, :] = kc_ref[...]
    k_scr[half + tq:wk, :] = kn_ref[...]
    v_scr[0:half, :] = vp_ref[...]
    v_scr[half:half + tq, :] = vc_ref[...]
    v_scr[half + tq:wk, :] = vn_ref[...]

    scale = HEAD_DIM ** -0.5
    qb = 2 * half
    wkb = qb + 2 * half
    lane = lax.broadcasted_iota(jnp.int32, (qb, LANES), 1)
    for q0 in range(0, tq, qb):
        qs, ks = slice(q0, q0 + qb), slice(q0, q0 + wkb)
        kpos = row0 + q0 - half + lax.broadcasted_iota(jnp.int32, (1, wkb), 1)
        edge = jnp.where((kpos >= lo) & (kpos < hi), 0.0, NEG_BIG).astype(F32)
        lse_all = jnp.zeros((qb, LANES), F32)
        for h in range(ATTN_HEADS):
            cs = slice(h * HEAD_DIM, (h + 1) * HEAD_DIM)
            z = lax.dot_general(q_ref[qs, cs], k_scr[ks, cs], (((1,), (1,)), ((), ())),
                                preferred_element_type=F32)
            z = z + bias_ref[h] + edge
            m = jnp.max(z, axis=-1, keepdims=True)
            p = jnp.exp2((z - m) * (scale * LOG2E))
            den = jnp.sum(p, axis=-1, keepdims=True)
            pv = jnp.dot(p.astype(BF16), v_scr[ks, cs], preferred_element_type=F32)
            o_ref[qs, cs] = (pv * (1.0 / den)).astype(o_ref.dtype)
            lse_all = jnp.where(lane == h, scale * m + jnp.log(den), lse_all)
        lse_ref[qs, :] = lse_all


def _band_bias(half, dil, slopes):
    qb, wk = 2 * half, 4 * half
    rel = np.arange(wk)[None, :] - half - np.arange(qb)[:, None]
    dist = (np.abs(rel) * dil).astype(np.float32)
    scale = np.float32(HEAD_DIM ** -0.5)
    bias = -(slopes.astype(np.float32)[:, None, None] * dist[None]) / scale
    return np.where((np.abs(rel) <= half)[None], bias, np.float32(NEG_BIG)).astype(np.float32)


def _band_attention(qkv, *, dil, window, slopes, layout, name):
    t = qkv.shape[0]
    hd_all = ATTN_HEADS * HEAD_DIM
    half = window // (2 * dil)
    tq = min(TILES["attn_tq"], min(layout[1:]) // dil)
    wk = tq + 2 * half
    rows = t // dil
    nt = rows // tq
    per = tq // half
    nhalf = t // half
    lay = tuple(v // dil for v in layout)
    bias = jnp.asarray(_band_bias(half, dil, slopes))

    def cur(col):
        return pl.BlockSpec((tq, hd_all), lambda r, j: (r * nt + j, col))

    def prev(col):
        return pl.BlockSpec((half, hd_all), lambda r, j: (jnp.maximum((r * nt + j) * per - 1, 0), col))

    def nxt(col):
        return pl.BlockSpec((half, hd_all), lambda r, j: (jnp.minimum((r * nt + j + 1) * per, nhalf - 1), col))

    return pl.pallas_call(
        functools.partial(_attn_kernel, tq=tq, half=half, layout=lay),
        grid=(dil, nt),
        in_specs=[cur(0), prev(1), cur(1), nxt(1), prev(2), cur(2), nxt(2),
                  _resident(bias.shape, lambda r, j: (0, 0, 0))],
        out_specs=[pl.BlockSpec((tq, hd_all), lambda r, j: (r * nt + j, 0)),
                   pl.BlockSpec((tq, LANES), lambda r, j: (r * nt + j, 0))],
        out_shape=[jax.ShapeDtypeStruct((t, hd_all), BF16), jax.ShapeDtypeStruct((t, LANES), F32)],
        scratch_shapes=[pltpu.VMEM((wk, hd_all), BF16), pltpu.VMEM((wk, hd_all), BF16)],
        compiler_params=_cparams("parallel", "parallel"),
        name=name,
    )(qkv, qkv, qkv, qkv, qkv, qkv, qkv, bias)


def _combine_ln_kernel(o0_ref, o1_ref, o2_ref, l0_ref, l1_ref, l2_ref, p1_ref, p2_ref, w_ref, r_ref,
                       g_ref, b_ref, o_ref, ob_ref, mix_ref, *, alpha, rows):
    bm, k = o0_ref.shape

    def unpermute(perm_ref, o_blk_ref, l_blk_ref, r0):
        dil = o_blk_ref.shape[0]
        run = slice(r0 // dil, (r0 + rows) // dil)
        o_nat = jnp.dot(perm_ref[...], o_blk_ref[:, run, :].reshape(rows, k), preferred_element_type=F32)
        l3 = jnp.dot(perm_ref[...], jnp.concatenate(_split3(l_blk_ref[:, run, :].reshape(rows, LANES)), axis=1),
                     preferred_element_type=F32)
        return o_nat, l3[:, 0:LANES] + l3[:, LANES:2 * LANES] + l3[:, 2 * LANES:3 * LANES]

    for r0 in range(0, bm, rows):
        rs = slice(r0, r0 + rows)
        o1, l1 = unpermute(p1_ref, o1_ref, l1_ref, r0)
        o2, l2 = unpermute(p2_ref, o2_ref, l2_ref, r0)
        l0 = l0_ref[rs, :]
        m = jnp.maximum(jnp.maximum(l0, l1), l2)
        e0, e1, e2 = jnp.exp(l0 - m), jnp.exp(l1 - m), jnp.exp(l2 - m)
        inv = 1.0 / (e0 + e1 + e2)
        w0, w1, w2 = e0 * inv, e1 * inv, e2 * inv
        for h in range(ATTN_HEADS):
            cs = slice(h * HEAD_DIM, (h + 1) * HEAD_DIM)
            mix = (w0[:, h:h + 1] * o0_ref[rs, cs].astype(F32) + w1[:, h:h + 1] * o1[:, cs]
                   + w2[:, h:h + 1] * o2[:, cs])
            mix_ref[rs, cs] = mix.astype(BF16)
        acc = jnp.dot(mix_ref[rs, :], w_ref[...], preferred_element_type=F32)
        y = _layer_norm_rows(alpha * r_ref[rs, :] + acc, g_ref[...], b_ref[...])
        o_ref[rs, :] = y
        ob_ref[rs, :] = y.astype(BF16)


def _unpermute_matrix(bm, dil):
    p = np.zeros((bm, bm), np.float32)
    sub = np.arange(bm)
    r, i = sub // (bm // dil), sub % (bm // dil)
    p[i * dil + r, sub] = 1.0
    return p


def _combine_proj_ln(outs, lses, w, layer, res, g, b, *, alpha, name):
    t, d = res.shape
    k = w.shape[1]
    bm = TILES["mix_bm"]
    dils = [dil for _, dil in DILATED_CONFIGS]
    row = lambda i: (i, 0)
    const = lambda i: (0, 0)

    def sub_major(a, dil):
        return a.reshape(dil, t // dil, a.shape[1])

    def sub_spec(cols, dil):
        return pl.BlockSpec((dil, bm // dil, cols), lambda i: (0, i, 0))

    rows = TILES["out_rows"]
    perms = [jnp.asarray(_unpermute_matrix(rows, dil), dtype=BF16) for dil in dils[1:]]
    return pl.pallas_call(
        functools.partial(_combine_ln_kernel, alpha=alpha, rows=rows),
        grid=(t // bm,),
        in_specs=[pl.BlockSpec((bm, k), row), sub_spec(k, dils[1]), sub_spec(k, dils[2]),
                  pl.BlockSpec((bm, LANES), row), sub_spec(LANES, dils[1]), sub_spec(LANES, dils[2]),
                  _resident((rows, rows), const), _resident((rows, rows), const),
                  _resident((None, k, d), lambda i: (layer, 0, 0)),
                  pl.BlockSpec((bm, d), row), _resident((1, d), const), _resident((1, d), const)],
        out_specs=[pl.BlockSpec((bm, d), row), pl.BlockSpec((bm, d), row)],
        out_shape=[jax.ShapeDtypeStruct((t, d), F32), jax.ShapeDtypeStruct((t, d), BF16)],
        scratch_shapes=[pltpu.VMEM((bm, k), BF16)],
        compiler_params=_cparams("parallel"),
        name=name,
    )(outs[0], sub_major(outs[1], dils[1]), sub_major(outs[2], dils[2]),
      lses[0], sub_major(lses[1], dils[1]), sub_major(lses[2], dils[2]),
      perms[0], perms[1], w, res, g, b)


def _conv_kernel(p_ref, c_ref, n_ref, s_ref, w_ref, b_ref, o_ref, ext_ref, *, tr, layout):
    i = pl.program_id(0)
    row0 = i * tr
    lo, hi = _seq_bounds(row0, layout)
    halo = BF16_ROWS
    zero = jnp.zeros_like(p_ref)
    ext_ref[0:halo, :] = jnp.where(row0 == lo, zero, p_ref[...])
    ext_ref[halo:halo + tr, :] = c_ref[...]
    ext_ref[halo + tr:tr + 2 * halo, :] = jnp.where(row0 + tr == hi, zero, n_ref[...])
    rb = CONV_ROWS
    mid = D_CONV // 2
    taps = [kk for kk in range(D_CONV) if kk != mid]
    for r0 in range(0, tr, rb):
        blk = ext_ref[r0:r0 + rb + 2 * halo, :]
        shifted = jnp.dot(s_ref[...], blk, preferred_element_type=F32)
        acc = b_ref[...] + w_ref[mid:mid + 1, :] * blk[halo:halo + rb, :].astype(F32)
        for j, kk in enumerate(taps):
            acc = acc + w_ref[kk:kk + 1, :] * shifted[j * rb:(j + 1) * rb, :]
        o_ref[r0:r0 + rb, :] = (acc * (1.0 / (1.0 + jnp.exp2(acc * (-LOG2E))))).astype(o_ref.dtype)


def _shift_matrix(rb, halo):
    offs = [kk - D_CONV // 2 for kk in range(D_CONV) if kk != D_CONV // 2]
    s = np.zeros((len(offs) * rb, rb + 2 * halo), np.float32)
    for j, off in enumerate(offs):
        s[j * rb + np.arange(rb), halo + np.arange(rb) + off] = 1.0
    return s


def _conv_silu(u, w, b, layer, *, layout, name):
    t, c = u.shape
    tr, bc = TILES["conv_tr"], TILES["conv_bc"]
    per = tr // BF16_ROWS
    nh = t // BF16_ROWS
    shift = jnp.asarray(_shift_matrix(CONV_ROWS, BF16_ROWS), dtype=BF16)
    return pl.pallas_call(
        functools.partial(_conv_kernel, tr=tr, layout=layout),
        grid=(t // tr, c // bc),
        in_specs=[pl.BlockSpec((BF16_ROWS, bc), lambda i, j: (jnp.maximum(i * per - 1, 0), j)),
                  pl.BlockSpec((tr, bc), lambda i, j: (i, j)),
                  pl.BlockSpec((BF16_ROWS, bc), lambda i, j: (jnp.minimum((i + 1) * per, nh - 1), j)),
                  _resident(shift.shape, lambda i, j: (0, 0)),
                  pl.BlockSpec((None, D_CONV, bc), lambda i, j: (layer, 0, j)),
                  pl.BlockSpec((None, 1, bc), lambda i, j: (layer, 0, j))],
        out_specs=pl.BlockSpec((tr, bc), lambda i, j: (i, j)),
        out_shape=jax.ShapeDtypeStruct((t, c), BF16),
        scratch_shapes=[pltpu.VMEM((tr + 2 * BF16_ROWS, bc), BF16)],
        compiler_params=_cparams("parallel", "parallel"),
        name=name,
    )(u, u, u, shift, w, b)


def _cumsum_rows(tri, la):
    out = jnp.dot(tri, jnp.concatenate(_split3(la), axis=1), preferred_element_type=F32)
    return out[:, 0:LANES] + out[:, LANES:2 * LANES] + out[:, 2 * LANES:3 * LANES]


def _expand_heads(q, e3_ref):
    return jnp.dot(jnp.concatenate(_split3(q), axis=1), e3_ref[...], preferred_element_type=F32)


def _softplus(x):
    return jnp.maximum(x, 0.0) + jnp.log(1.0 + jnp.exp(-jnp.abs(x)))


def _chunk_flags(c, layout_chunks):
    lo, hi = _seq_bounds(c, layout_chunks)
    return c == lo, c == hi - 1


def _ssd_bwd_kernel(x_ref, b_ref, dt_ref, dtb_ref, alog_ref, e3_ref, prev_ref, state_ref,
                    *, nchunks, layout_chunks, nheads):
    c = nchunks - 1 - pl.program_id(0)
    _, is_last = _chunk_flags(c, layout_chunks)

    @pl.when(is_last)
    def _():
        state_ref[...] = jnp.zeros_like(state_ref)

    prev_ref[...] = state_ref[...].astype(BF16)

    dt = _softplus(dt_ref[...] + dtb_ref[...])
    la = dt * (-jnp.exp(alog_ref[...]))
    ti = lax.broadcasted_iota(jnp.int32, (CHUNK, CHUNK), 0)
    si = lax.broadcasted_iota(jnp.int32, (CHUNK, CHUNK), 1)
    triu = jnp.where(si >= ti, 1.0, 0.0).astype(BF16)
    acs = _cumsum_rows(triu, la)[:, nheads:2 * nheads]
    dt_b = dt[:, nheads:2 * nheads]
    wgt = dt_b * jnp.exp(acs[0:1, :] - acs)
    xdec = (x_ref[...] * _expand_heads(wgt, e3_ref)).astype(BF16)
    cdec = _expand_heads(jnp.broadcast_to(jnp.exp(acs[0:1, :]), (SUBLANES, nheads)), e3_ref)[0:1, :]
    gw = (nheads // SSD_GROUPS) * SSD_HEADDIM
    for g in range(SSD_GROUPS):
        cs = slice(g * gw, (g + 1) * gw)
        bg = b_ref[:, g * D_STATE:(g + 1) * D_STATE]
        st = lax.dot_general(bg, xdec[:, cs], (((0,), (0,)), ((), ())), preferred_element_type=F32)
        state_ref[:, cs] = state_ref[:, cs] * cdec[:, cs] + st


def _ssd_fwd_kernel(x_ref, z_ref, b_ref, c_ref, dt_ref, prevb_ref, dtb_ref, alog_ref, dskip_ref,
                    nw_ref, e3_ref, o_ref, state_ref, y_ref, *, layout_chunks, nheads):
    c = pl.program_id(0)
    is_first, _ = _chunk_flags(c, layout_chunks)

    @pl.when(is_first)
    def _():
        state_ref[...] = jnp.zeros_like(state_ref)

    dt = _softplus(dt_ref[...] + dtb_ref[...])
    la = dt * (-jnp.exp(alog_ref[...]))
    ti = lax.broadcasted_iota(jnp.int32, (CHUNK, CHUNK), 0)
    si = lax.broadcasted_iota(jnp.int32, (CHUNK, CHUNK), 1)
    lower = ti >= si
    upper = si >= ti
    lanes2h = lax.broadcasted_iota(jnp.int32, (CHUNK, 2 * nheads), 1)
    acs = jnp.where(lanes2h < nheads,
                    _cumsum_rows(jnp.where(lower, 1.0, 0.0).astype(BF16), la),
                    _cumsum_rows(jnp.where(upper, 1.0, 0.0).astype(BF16), la))
    col2 = acs * LOG2E
    row2_t = (jnp.log(dt) * LOG2E - col2).T
    acs_f, acs_b = acs[:, 0:nheads], acs[:, nheads:2 * nheads]
    e_f = _expand_heads(jnp.exp(acs_f), e3_ref)
    e_b = _expand_heads(jnp.exp(acs_b), e3_ref)
    w_f = dt[:, 0:nheads] * jnp.exp(acs_f[CHUNK - 1:CHUNK, :] - acs_f)
    x = x_ref[...].astype(F32)
    xdec = (x * _expand_heads(w_f, e3_ref)).astype(BF16)
    cdec = e_f[CHUNK - 1:CHUNK, :]

    hpg = nheads // SSD_GROUPS
    gw = hpg * SSD_HEADDIM
    lane = lax.broadcasted_iota(jnp.int32, (CHUNK, LANES), 1)
    left = (lane < SSD_HEADDIM)
    for g in range(SSD_GROUPS):
        cs = slice(g * gw, (g + 1) * gw)
        bg = b_ref[:, g * D_STATE:(g + 1) * D_STATE]
        cg = c_ref[:, g * D_STATE:(g + 1) * D_STATE]
        cb = lax.dot_general(cg, bg, (((1,), (1,)), ((), ())), preferred_element_type=F32)
        y_off = (jnp.dot(cg, state_ref[:, cs].astype(BF16), preferred_element_type=F32) * e_f[:, cs]
                 + jnp.dot(cg, prevb_ref[:, cs], preferred_element_type=F32) * e_b[:, cs])
        for pr in range(hpg // 2):
            mats = []
            for h in (g * hpg + 2 * pr, g * hpg + 2 * pr + 1):
                hb = nheads + h
                seg_f = col2[:, h:h + 1] + row2_t[h:h + 1, :]
                seg_b = col2[:, hb:hb + 1] + row2_t[hb:hb + 1, :]
                dec = jnp.exp2(jnp.where(lower, seg_f, NEG_BIG)) + jnp.exp2(jnp.where(upper, seg_b, NEG_BIG))
                mats.append((cb * dec).astype(BF16))
            ps = slice(g * gw + pr * LANES, g * gw + (pr + 1) * LANES)
            xp = x[:, ps]
            rhs = jnp.concatenate([jnp.where(left, xp, 0.0), jnp.where(left, 0.0, xp)], axis=0).astype(BF16)
            y_diag = jnp.dot(jnp.concatenate(mats, axis=1), rhs, preferred_element_type=F32)
            po = slice(pr * LANES, (pr + 1) * LANES)
            y_ref[:, ps] = y_diag + y_off[:, po] + dskip_ref[:, ps] * xp
        st = lax.dot_general(bg, xdec[:, cs], (((0,), (0,)), ((), ())), preferred_element_type=F32)
        state_ref[:, cs] = state_ref[:, cs] * cdec[:, cs] + st

    z = z_ref[...].astype(F32)
    u = y_ref[...] * (z * (1.0 / (1.0 + jnp.exp(-z))))
    for g in range(SSD_GROUPS):
        cs = slice(g * gw, (g + 1) * gw)
        ug = u[:, cs]
        ms = jnp.mean(ug * ug, axis=-1, keepdims=True)
        o_ref[:, cs] = (ug * lax.rsqrt(ms + LN_EPS) * nw_ref[:, cs]).astype(o_ref.dtype)


def _ssd_scan_norm(xbc, z, dt_raw, dtb, alog, dsk, nw, layer, *, layout, name):
    t, d_inner = z.shape
    nheads = d_inner // SSD_HEADDIM
    gn = SSD_GROUPS * D_STATE
    b_col, c_col = d_inner // gn, d_inner // gn + 1
    nchunks = t // CHUNK
    layout_chunks = tuple(v // CHUNK for v in layout)
    expand = np.repeat(np.eye(nheads, dtype=np.float32), SSD_HEADDIM, axis=1)
    e3 = jnp.asarray(np.concatenate([expand] * 3, axis=0), dtype=BF16)
    const = lambda c: (0, 0)
    par = lambda c: (layer, 0, 0)
    rev = lambda c: (nchunks - 1 - c, 0)
    fwd = lambda c: (c, 0)

    prev_b = pl.pallas_call(
        functools.partial(_ssd_bwd_kernel, nchunks=nchunks, layout_chunks=layout_chunks, nheads=nheads),
        grid=(nchunks,),
        in_specs=[pl.BlockSpec((CHUNK, d_inner), rev),
                  pl.BlockSpec((CHUNK, gn), lambda c: (nchunks - 1 - c, b_col)),
                  pl.BlockSpec((CHUNK, 2 * nheads), rev),
                  _resident((None, 1, 2 * nheads), par),
                  _resident((None, 1, 2 * nheads), par),
                  _resident((3 * nheads, d_inner), const)],
        out_specs=pl.BlockSpec((D_STATE, d_inner), rev),
        out_shape=jax.ShapeDtypeStruct((nchunks * D_STATE, d_inner), BF16),
        scratch_shapes=[pltpu.VMEM((D_STATE, d_inner), F32)],
        compiler_params=_cparams("arbitrary"),
        name=name + "_bwd",
    )(xbc, xbc, dt_raw, dtb, alog, e3)

    return pl.pallas_call(
        functools.partial(_ssd_fwd_kernel, layout_chunks=layout_chunks, nheads=nheads),
        grid=(nchunks,),
        in_specs=[pl.BlockSpec((CHUNK, d_inner), fwd),
                  pl.BlockSpec((CHUNK, d_inner), fwd),
                  pl.BlockSpec((CHUNK, gn), lambda c: (c, b_col)),
                  pl.BlockSpec((CHUNK, gn), lambda c: (c, c_col)),
                  pl.BlockSpec((CHUNK, 2 * nheads), fwd),
                  pl.BlockSpec((D_STATE, d_inner), fwd),
                  _resident((None, 1, 2 * nheads), par),
                  _resident((None, 1, 2 * nheads), par),
                  _resident((None, 1, d_inner), par),
                  _resident((None, 1, d_inner), par),
                  _resident((3 * nheads, d_inner), const)],
        out_specs=pl.BlockSpec((CHUNK, d_inner), fwd),
        out_shape=jax.ShapeDtypeStruct((t, d_inner), BF16),
        scratch_shapes=[pltpu.VMEM((D_STATE, d_inner), F32), pltpu.VMEM((CHUNK, d_inner), F32)],
        compiler_params=_cparams("arbitrary"),
        name=name + "_fwd",
    )(xbc, z, xbc, xbc, dt_raw, prev_b, dtb, alog, dsk, nw, e3)


def _sub_major_kernel(x_ref, *refs, rows):
    nd = len(refs) // 2
    bm, d = x_ref.shape
    for r0 in range(0, bm, rows):
        blk = x_ref[r0:r0 + rows, :]
        for p_ref, o_ref in zip(refs[:nd], refs[nd:]):
            dil = o_ref.shape[0]
            sub = jnp.dot(p_ref[...], blk, preferred_element_type=F32).astype(BF16)
            o_ref[:, r0 // dil:(r0 + rows) // dil, :] = sub.reshape(dil, rows // dil, d)


def _to_sub_major(xb, dils):
    t, d = xb.shape
    bm, rows = TILES["mix_bm"], TILES["out_rows"]
    perms = [jnp.asarray(_unpermute_matrix(rows, dil).T, dtype=BF16) for dil in dils]
    outs = pl.pallas_call(
        functools.partial(_sub_major_kernel, rows=rows),
        grid=(t // bm,),
        in_specs=[pl.BlockSpec((bm, d), lambda i: (i, 0))]
                 + [_resident((rows, rows), lambda i: (0, 0)) for _ in dils],
        out_specs=[pl.BlockSpec((dil, bm // dil, d), lambda i: (0, i, 0)) for dil in dils],
        out_shape=[jax.ShapeDtypeStruct((dil, t // dil, d), BF16) for dil in dils],
        compiler_params=_cparams("parallel"),
        name="sub_major",
    )(xb, *perms)
    return [o.reshape(t, d) for o in outs]


def _alibi_slopes(n):
    return np.float32(2.0) ** (np.float32(-8.0) * np.arange(1, n + 1, dtype=np.float32) / np.float32(n))


def _attention_layer(x, xb, w_qkv, w_o, layer, g, b, *, alpha, layout):
    ngroups = len(DILATED_CONFIGS)
    hd_all = ATTN_HEADS * HEAD_DIM
    slopes = _alibi_slopes(ngroups * ATTN_HEADS).reshape(ngroups, ATTN_HEADS)
    outs, lses = [], []
    assert DILATED_CONFIGS[0][1] == 1
    xgs = [xb] + _to_sub_major(xb, [dil for _, dil in DILATED_CONFIGS[1:]])
    for gi, (window, dil) in enumerate(DILATED_CONFIGS):
        qkv = _matmul(xgs[gi], w_qkv, layer, gi * 3 * hd_all, 3 * hd_all, BF16, name=f"qkv_g{gi}")
        o, lse = _band_attention(qkv, dil=dil, window=window, slopes=slopes[gi], layout=layout, name=f"attn_g{gi}")
        outs.append(o)
        lses.append(lse)
    return _combine_proj_ln(outs, lses, w_o, layer, x, g, b, alpha=alpha, name="attn_out")


def _ssd_layer(x, xb, w_in, conv_w, conv_b, dtb, alog, dsk, nw, w_out, layer, g, b, *, alpha, layout):
    d_inner = w_out.shape[1]
    nheads = d_inner // SSD_HEADDIM
    gn = SSD_GROUPS * D_STATE
    conv_dim = d_inner + 2 * gn
    z = _matmul(xb, w_in, layer, 0, d_inner, BF16, name="ssd_in_z")
    xbc = _matmul(xb, w_in, layer, d_inner, conv_dim, BF16, name="ssd_in_xbc")
    dt_raw = _matmul(xb, w_in, layer, d_inner + conv_dim, 2 * nheads, F32, name="ssd_in_dt")
    xbc = _conv_silu(xbc, conv_w, conv_b, layer, layout=layout, name="ssd_conv")
    y = _ssd_scan_norm(xbc, z, dt_raw, dtb, alog, dsk, nw, layer, layout=layout, name="ssd_scan")
    return _matmul_res_ln(y, w_out, layer, x, g, b, alpha=alpha, name="ssd_out")


def kernel(x_prompt, x_sample, attn_w_qkv, attn_w_o, ssd_w_in, ssd_conv_w, ssd_conv_b, ssd_dt_bias, ssd_a_log,
           ssd_d, ssd_norm_w, ssd_w_out, mlp_w1, mlp_w2, ln_g, ln_b):
    bp, sp, d = x_prompt.shape
    bs, ss, _ = x_sample.shape
    tp, ts = bp * sp, bs * ss
    depth = mlp_w1.shape[0]
    nssd, d_inner = ssd_w_out.shape[0], ssd_w_out.shape[1]
    nheads = d_inner // SSD_HEADDIM
    alpha = (2.0 * depth) ** 0.25
    layout = (tp, sp, ss)
    x = jnp.concatenate([x_prompt.reshape(tp, d), x_sample.reshape(ts, d)], axis=0)
    xb = x.astype(BF16)
    w_qkv, w_o = attn_w_qkv, attn_w_o.astype(BF16)
    w_in, w_out = ssd_w_in, ssd_w_out.astype(BF16)
    w1, w2 = mlp_w1, mlp_w2.astype(BF16)
    conv_b = ssd_conv_b.reshape(nssd, 1, -1)
    dtb = ssd_dt_bias.reshape(nssd, 1, 2 * nheads).astype(F32)
    alog = ssd_a_log.reshape(nssd, 1, 2 * nheads).astype(F32)
    dsk = jnp.repeat(ssd_d.astype(F32), SSD_HEADDIM, axis=1).reshape(nssd, 1, d_inner)
    nw = ssd_norm_w.reshape(nssd, 1, d_inner).astype(F32)
    for i in range(depth):
        j = i // 2
        g0, b0 = ln_g[i, 0].reshape(1, d), ln_b[i, 0].reshape(1, d)
        g1, b1 = ln_g[i, 1].reshape(1, d), ln_b[i, 1].reshape(1, d)
        if i % 2 == 0:
            x, xb = _attention_layer(x, xb, w_qkv, w_o, j, g0, b0, alpha=alpha, layout=layout)
        else:
            x, xb = _ssd_layer(x, xb, w_in, ssd_conv_w, conv_b, dtb, alog, dsk, nw, w_out, j, g0, b0,
                               alpha=alpha, layout=layout)
        hid = _matmul(xb, w1, i, 0, w1.shape[2], BF16, act="relu2", name="mlp_up")
        x, xb = _matmul_res_ln(hid, w2, i, x, g1, b1, alpha=alpha, name="mlp_down")
    return x[:tp].reshape(bp, sp, d), x[tp:].reshape(bs, ss, d)
```

```python
import functools
import math

import numpy as np
import jax
import jax.numpy as jnp
from jax import lax
from jax.experimental import pallas as pl
from jax.experimental.pallas import tpu as pltpu

F32 = jnp.float32
BF16 = jnp.bfloat16

DILATED_CONFIGS = ((128, 1), (512, 4), (2048, 16))
ATTN_HEADS = 16
HEAD_DIM = 128
SSD_HEADDIM = 64
SSD_GROUPS = 8
D_STATE = 128
D_CONV = 5
CHUNK = 128
LN_EPS = 1e-5
NEG_BIG = -1e30
LOG2E = 1.4426950408889634

V7X_VMEM_LIMIT_BYTES = 56 * 1024 * 1024
LANES = 128
SUBLANES = 8
BF16_ROWS = 16
CONV_ROWS = 128

TILES = dict(
    proj_bm=1024, proj_bn=1024,
    out_bm=1024, out_bk=1024,
    out_rows=256,
    mix_bm=512,
    attn_tq=256,
    attn_qb=128,
    conv_tr=512, conv_bc=1024,
)


def _cparams(*sem):
    return pltpu.CompilerParams(dimension_semantics=sem, vmem_limit_bytes=V7X_VMEM_LIMIT_BYTES)


def _resident(block_shape, index_map):
    return pl.BlockSpec(block_shape, index_map, pipeline_mode=pl.Buffered(1))


def _layer_norm_rows(y, g, b):
    mu = jnp.mean(y, axis=-1, keepdims=True)
    yc = y - mu
    var = jnp.mean(yc * yc, axis=-1, keepdims=True)
    return yc * lax.rsqrt(var + LN_EPS) * g + b


def _seq_bounds(row, layout):
    rows_p, seq_p, seq_s = layout
    in_p = row < rows_p
    lo_p = (row // seq_p) * seq_p
    lo_s = rows_p + ((row - rows_p) // seq_s) * seq_s
    lo = jnp.where(in_p, lo_p, lo_s)
    return lo, lo + jnp.where(in_p, seq_p, seq_s)


def _split3(x):
    hi = x.astype(BF16)
    r1 = x - hi.astype(F32)
    mid = r1.astype(BF16)
    lo = (r1 - mid.astype(F32)).astype(BF16)
    return hi, mid, lo


def _mm_kernel(x_ref, w_ref, o_ref, wb_ref, *, act):
    @pl.when(pl.program_id(1) == 0)
    def _():
        wb_ref[...] = w_ref[...].astype(BF16)

    acc = jnp.dot(x_ref[...], wb_ref[...], preferred_element_type=F32)
    if act == "relu2":
        acc = jnp.maximum(acc, 0.0)
        acc = acc * acc
    o_ref[...] = acc.astype(o_ref.dtype)


def _matmul(x, w, layer, col0, n, out_dtype, *, act=None, name):
    t, k = x.shape
    bm = TILES["proj_bm"]
    bn = min(TILES["proj_bn"], n)
    c0 = col0 // bn
    return pl.pallas_call(
        functools.partial(_mm_kernel, act=act),
        grid=(n // bn, t // bm),
        in_specs=[pl.BlockSpec((bm, k), lambda j, i: (i, 0)),
                  pl.BlockSpec((None, k, bn), lambda j, i: (layer, 0, c0 + j))],
        out_specs=pl.BlockSpec((bm, bn), lambda j, i: (i, j)),
        out_shape=jax.ShapeDtypeStruct((t, n), out_dtype),
        scratch_shapes=[pltpu.VMEM((k, bn), BF16)],
        compiler_params=_cparams("parallel", "arbitrary"),
        name=name,
    )(x, w)


def _mm_ln_kernel(l_ref, w_ref, r_ref, g_ref, b_ref, o_ref, ob_ref, *, nk, alpha, rows):
    kk = pl.program_id(1)

    if nk > 1:
        @pl.when(kk == 0)
        def _():
            o_ref[...] = jnp.dot(l_ref[...], w_ref[...], preferred_element_type=F32)

        @pl.when((kk > 0) & (kk < nk - 1))
        def _():
            o_ref[...] += jnp.dot(l_ref[...], w_ref[...], preferred_element_type=F32)

    @pl.when(kk == nk - 1)
    def _():
        bm = l_ref.shape[0]
        for r0 in range(0, bm, rows):
            rs = slice(r0, r0 + rows)
            acc = jnp.dot(l_ref[rs, :], w_ref[...], preferred_element_type=F32)
            if nk > 1:
                acc = o_ref[rs, :] + acc
            y = _layer_norm_rows(alpha * r_ref[rs, :] + acc, g_ref[...], b_ref[...])
            o_ref[rs, :] = y
            ob_ref[rs, :] = y.astype(BF16)


def _matmul_res_ln(lhs, w, layer, res, g, b, *, alpha, name):
    t, k = lhs.shape
    n = w.shape[2]
    bm, bk = TILES["out_bm"], TILES["out_bk"]
    nk = k // bk
    return pl.pallas_call(
        functools.partial(_mm_ln_kernel, nk=nk, alpha=alpha, rows=TILES["out_rows"]),
        grid=(t // bm, nk),
        in_specs=[pl.BlockSpec((bm, bk), lambda i, kk: (i, kk)),
                  pl.BlockSpec((None, bk, n), lambda i, kk: (layer, kk, 0)),
                  pl.BlockSpec((bm, n), lambda i, kk: (i, 0)),
                  _resident((1, n), lambda i, kk: (0, 0)),
                  _resident((1, n), lambda i, kk: (0, 0))],
        out_specs=[pl.BlockSpec((bm, n), lambda i, kk: (i, 0)),
                   pl.BlockSpec((bm, n), lambda i, kk: (i, 0), pipeline_mode=pl.Buffered(1))],
        out_shape=[jax.ShapeDtypeStruct((t, n), F32), jax.ShapeDtypeStruct((t, n), BF16)],
        compiler_params=_cparams("parallel", "arbitrary"),
        name=name,
    )(lhs, w, res, g, b)


def _attn_kernel(q_ref, kp_ref, kc_ref, kn_ref, vp_ref, vc_ref, vn_ref, bias_ref, o_ref, lse_ref,
                 k_scr, v_scr, *, tq, half, layout):
    j = pl.program_id(1)
    row0 = j * tq
    lo, hi = _seq_bounds(row0, layout)
    wk = tq + 2 * half
    k_scr[0:half, :] = kp_ref[...]
    k_scr[half:half + tq, :] = kc_ref[...]
    k_scr[half + tq:wk, :] = kn_ref[...]
    v_scr[0:half, :] = vp_ref[...]
    v_scr[half:half + tq, :] = vc_ref[...]
    v_scr[half + tq:wk, :] = vn_ref[...]

    scale = HEAD_DIM ** -0.5
    qb = bias_ref.shape[1]
    wkb = qb + 2 * half
    lane = lax.broadcasted_iota(jnp.int32, (qb, LANES), 1)
    for q0 in range(0, tq, qb):
        qs, ks = slice(q0, q0 + qb), slice(q0, q0 + wkb)
        kpos = row0 + q0 - half + lax.broadcasted_iota(jnp.int32, (1, wkb), 1)
        edge = jnp.where((kpos >= lo) & (kpos < hi), 0.0, NEG_BIG).astype(F32)
        lse_all = jnp.zeros((qb, LANES), F32)
        for h in range(ATTN_HEADS):
            cs = slice(h * HEAD_DIM, (h + 1) * HEAD_DIM)
            z = lax.dot_general(q_ref[qs, cs], k_scr[ks, cs], (((1,), (1,)), ((), ())),
                                preferred_element_type=F32)
            z = z + bias_ref[h] + edge
            m = jnp.max(z, axis=-1, keepdims=True)
            p = jnp.exp2((z - m) * (scale * LOG2E))
            den = jnp.sum(p, axis=-1, keepdims=True)
            pv = jnp.dot(p.astype(BF16), v_scr[ks, cs], preferred_element_type=F32)
            o_ref[qs, cs] = (pv * (1.0 / den)).astype(o_ref.dtype)
            lse_all = jnp.where(lane == h, scale * m + jnp.log(den), lse_all)
        lse_ref[qs, :] = lse_all


def _band_bias(qb, half, dil, slopes):
    wk = qb + 2 * half
    rel = np.arange(wk)[None, :] - half - np.arange(qb)[:, None]
    dist = (np.abs(rel) * dil).astype(np.float32)
    scale = np.float32(HEAD_DIM ** -0.5)
    bias = -(slopes.astype(np.float32)[:, None, None] * dist[None]) / scale
    return np.where((np.abs(rel) <= half)[None], bias, np.float32(NEG_BIG)).astype(np.float32)


def _band_attention(qkv, *, dil, window, slopes, layout, name):
    t = qkv.shape[0]
    hd_all = ATTN_HEADS * HEAD_DIM
    half = window // (2 * dil)
    tq = min(TILES["attn_tq"], min(layout[1:]) // dil)
    wk = tq + 2 * half
    rows = t // dil
    nt = rows // tq
    per = tq // half
    nhalf = t // half
    lay = tuple(v // dil for v in layout)
    bias = jnp.asarray(_band_bias(TILES["attn_qb"], half, dil, slopes))

    def cur(col):
        return pl.BlockSpec((tq, hd_all), lambda r, j: (r * nt + j, col))

    def prev(col):
        return pl.BlockSpec((half, hd_all), lambda r, j: (jnp.maximum((r * nt + j) * per - 1, 0), col))

    def nxt(col):
        return pl.BlockSpec((half, hd_all), lambda r, j: (jnp.minimum((r * nt + j + 1) * per, nhalf - 1), col))

    return pl.pallas_call(
        functools.partial(_attn_kernel, tq=tq, half=half, layout=lay),
        grid=(dil, nt),
        in_specs=[cur(0), prev(1), cur(1), nxt(1), prev(2), cur(2), nxt(2),
                  _resident(bias.shape, lambda r, j: (0, 0, 0))],
        out_specs=[pl.BlockSpec((tq, hd_all), lambda r, j: (r * nt + j, 0)),
                   pl.BlockSpec((tq, LANES), lambda r, j: (r * nt + j, 0))],
        out_shape=[jax.ShapeDtypeStruct((t, hd_all), BF16), jax.ShapeDtypeStruct((t, LANES), F32)],
        scratch_shapes=[pltpu.VMEM((wk, hd_all), BF16), pltpu.VMEM((wk, hd_all), BF16)],
        compiler_params=_cparams("parallel", "parallel"),
        name=name,
    )(qkv, qkv, qkv, qkv, qkv, qkv, qkv, bias)


def _mix_kernel(o0_ref, o1_ref, o2_ref, l0_ref, l1_ref, l2_ref, p1_ref, p2_ref, ex_ref, mix_ref, *, rows):
    bm, k = o0_ref.shape

    def unpermute(perm_ref, o_blk_ref, l_blk_ref, r0):
        dil = o_blk_ref.shape[0]
        run = slice(r0 // dil, (r0 + rows) // dil)
        o_nat = jnp.dot(perm_ref[...], o_blk_ref[:, run, :].reshape(rows, k), preferred_element_type=F32)
        l3 = jnp.dot(perm_ref[...], jnp.concatenate(_split3(l_blk_ref[:, run, :].reshape(rows, LANES)), axis=1),
                     preferred_element_type=F32)
        return o_nat, l3[:, 0:LANES] + l3[:, LANES:2 * LANES] + l3[:, 2 * LANES:3 * LANES]

    for r0 in range(0, bm, rows):
        rs = slice(r0, r0 + rows)
        o1, l1 = unpermute(p1_ref, o1_ref, l1_ref, r0)
        o2, l2 = unpermute(p2_ref, o2_ref, l2_ref, r0)
        l0 = l0_ref[rs, :]
        m = jnp.maximum(jnp.maximum(l0, l1), l2)
        e0, e1, e2 = jnp.exp(l0 - m), jnp.exp(l1 - m), jnp.exp(l2 - m)
        inv = 1.0 / (e0 + e1 + e2)

        def over_head_lanes(w):
            hi = w.astype(BF16)
            lo = (w - hi.astype(F32)).astype(BF16)
            return jnp.dot(jnp.concatenate([hi, lo], axis=1), ex_ref[...], preferred_element_type=F32)

        o0 = o0_ref[rs, :].astype(F32)
        mix = o0 + over_head_lanes(e1 * inv) * (o1 - o0) + over_head_lanes(e2 * inv) * (o2 - o0)
        mix_ref[rs, :] = mix.astype(BF16)


def _unpermute_matrix(bm, dil):
    p = np.zeros((bm, bm), np.float32)
    sub = np.arange(bm)
    r, i = sub // (bm // dil), sub % (bm // dil)
    p[i * dil + r, sub] = 1.0
    return p


def _mix_groups(outs, lses, *, name):
    t, k = outs[0].shape
    bm = TILES["mix_bm"]
    dils = [dil for _, dil in DILATED_CONFIGS]
    row = lambda i: (i, 0)
    const = lambda i: (0, 0)

    def sub_major(a, dil):
        return a.reshape(dil, t // dil, a.shape[1])

    def sub_spec(cols, dil):
        return pl.BlockSpec((dil, bm // dil, cols), lambda i: (0, i, 0))

    rows = TILES["out_rows"]
    perms = [jnp.asarray(_unpermute_matrix(rows, dil), dtype=BF16) for dil in dils[1:]]
    head_rows = np.zeros((LANES, k), np.float32)
    head_rows[:ATTN_HEADS] = np.repeat(np.eye(ATTN_HEADS, dtype=np.float32), HEAD_DIM, axis=1)
    expand = jnp.asarray(np.concatenate([head_rows, head_rows], axis=0), dtype=BF16)
    return pl.pallas_call(
        functools.partial(_mix_kernel, rows=rows),
        grid=(t // bm,),
        in_specs=[pl.BlockSpec((bm, k), row), sub_spec(k, dils[1]), sub_spec(k, dils[2]),
                  pl.BlockSpec((bm, LANES), row), sub_spec(LANES, dils[1]), sub_spec(LANES, dils[2]),
                  _resident((rows, rows), const), _resident((rows, rows), const),
                  _resident((2 * LANES, k), const)],
        out_specs=pl.BlockSpec((bm, k), row),
        out_shape=jax.ShapeDtypeStruct((t, k), BF16),
        compiler_params=_cparams("parallel"),
        name=name,
    )(outs[0], sub_major(outs[1], dils[1]), sub_major(outs[2], dils[2]),
      lses[0], sub_major(lses[1], dils[1]), sub_major(lses[2], dils[2]),
      perms[0], perms[1], expand)


def _conv_kernel(p_ref, c_ref, n_ref, s_ref, w_ref, b_ref, o_ref, ext_ref, *, tr, layout):
    i = pl.program_id(0)
    row0 = i * tr
    lo, hi = _seq_bounds(row0, layout)
    halo = BF16_ROWS
    zero = jnp.zeros_like(p_ref)
    ext_ref[0:halo, :] = jnp.where(row0 == lo, zero, p_ref[...])
    ext_ref[halo:halo + tr, :] = c_ref[...]
    ext_ref[halo + tr:tr + 2 * halo, :] = jnp.where(row0 + tr == hi, zero, n_ref[...])
    rb = CONV_ROWS
    mid = D_CONV // 2
    taps = [kk for kk in range(D_CONV) if kk != mid]
    for r0 in range(0, tr, rb):
        blk = ext_ref[r0:r0 + rb + 2 * halo, :]
        shifted = jnp.dot(s_ref[...], blk, preferred_element_type=F32)
        acc = b_ref[...] + w_ref[mid:mid + 1, :] * blk[halo:halo + rb, :].astype(F32)
        for j, kk in enumerate(taps):
            acc = acc + w_ref[kk:kk + 1, :] * shifted[j * rb:(j + 1) * rb, :]
        o_ref[r0:r0 + rb, :] = (acc * (1.0 / (1.0 + jnp.exp2(acc * (-LOG2E))))).astype(o_ref.dtype)


def _shift_matrix(rb, halo):
    offs = [kk - D_CONV // 2 for kk in range(D_CONV) if kk != D_CONV // 2]
    s = np.zeros((len(offs) * rb, rb + 2 * halo), np.float32)
    for j, off in enumerate(offs):
        s[j * rb + np.arange(rb), halo + np.arange(rb) + off] = 1.0
    return s


def _conv_silu(u, w, b, layer, *, layout, name):
    t, c = u.shape
    tr, bc = TILES["conv_tr"], TILES["conv_bc"]
    per = tr // BF16_ROWS
    nh = t // BF16_ROWS
    shift = jnp.asarray(_shift_matrix(CONV_ROWS, BF16_ROWS), dtype=BF16)
    return pl.pallas_call(
        functools.partial(_conv_kernel, tr=tr, layout=layout),
        grid=(t // tr, c // bc),
        in_specs=[pl.BlockSpec((BF16_ROWS, bc), lambda i, j: (jnp.maximum(i * per - 1, 0), j)),
                  pl.BlockSpec((tr, bc), lambda i, j: (i, j)),
                  pl.BlockSpec((BF16_ROWS, bc), lambda i, j: (jnp.minimum((i + 1) * per, nh - 1), j)),
                  _resident(shift.shape, lambda i, j: (0, 0)),
                  pl.BlockSpec((None, D_CONV, bc), lambda i, j: (layer, 0, j)),
                  pl.BlockSpec((None, 1, bc), lambda i, j: (layer, 0, j))],
        out_specs=pl.BlockSpec((tr, bc), lambda i, j: (i, j)),
        out_shape=jax.ShapeDtypeStruct((t, c), BF16),
        scratch_shapes=[pltpu.VMEM((tr + 2 * BF16_ROWS, bc), BF16)],
        compiler_params=_cparams("parallel", "parallel"),
        name=name,
    )(u, u, u, shift, w, b)


def _cumsum_rows(tri, la):
    out = jnp.dot(tri, jnp.concatenate(_split3(la), axis=1), preferred_element_type=F32)
    return out[:, 0:LANES] + out[:, LANES:2 * LANES] + out[:, 2 * LANES:3 * LANES]


def _expand_heads(q, e3_ref):
    return jnp.dot(jnp.concatenate(_split3(q), axis=1), e3_ref[...], preferred_element_type=F32)


def _softplus(x):
    return jnp.maximum(x, 0.0) + jnp.log(1.0 + jnp.exp(-jnp.abs(x)))


def _chunk_flags(c, layout_chunks):
    lo, hi = _seq_bounds(c, layout_chunks)
    return c == lo, c == hi - 1


def _ssd_bwd_kernel(x_ref, b_ref, dt_ref, dtb_ref, alog_ref, e3_ref, prev_ref, state_ref,
                    *, nchunks, layout_chunks, nheads):
    c = nchunks - 1 - pl.program_id(0)
    _, is_last = _chunk_flags(c, layout_chunks)

    @pl.when(is_last)
    def _():
        state_ref[...] = jnp.zeros_like(state_ref)

    prev_ref[...] = state_ref[...].astype(BF16)

    dt = _softplus(dt_ref[...] + dtb_ref[...])
    la = dt * (-jnp.exp(alog_ref[...]))
    ti = lax.broadcasted_iota(jnp.int32, (CHUNK, CHUNK), 0)
    si = lax.broadcasted_iota(jnp.int32, (CHUNK, CHUNK), 1)
    triu = jnp.where(si >= ti, 1.0, 0.0).astype(BF16)
    acs = _cumsum_rows(triu, la)[:, nheads:2 * nheads]
    dt_b = dt[:, nheads:2 * nheads]
    wgt = dt_b * jnp.exp(acs[0:1, :] - acs)
    xdec = (x_ref[...] * _expand_heads(wgt, e3_ref)).astype(BF16)
    cdec = _expand_heads(jnp.broadcast_to(jnp.exp(acs[0:1, :]), (SUBLANES, nheads)), e3_ref)[0:1, :]
    gw = (nheads // SSD_GROUPS) * SSD_HEADDIM
    for g in range(SSD_GROUPS):
        cs = slice(g * gw, (g + 1) * gw)
        bg = b_ref[:, g * D_STATE:(g + 1) * D_STATE]
        st = lax.dot_general(bg, xdec[:, cs], (((0,), (0,)), ((), ())), preferred_element_type=F32)
        state_ref[:, cs] = state_ref[:, cs] * cdec[:, cs] + st


def _ssd_fwd_kernel(x_ref, z_ref, b_ref, c_ref, dt_ref, prevb_ref, dtb_ref, alog_ref, dskip_ref,
                    nw_ref, e3_ref, o_ref, state_ref, y_ref, *, layout_chunks, nheads):
    c = pl.program_id(0)
    is_first, _ = _chunk_flags(c, layout_chunks)

    @pl.when(is_first)
    def _():
        state_ref[...] = jnp.zeros_like(state_ref)

    dt = _softplus(dt_ref[...] + dtb_ref[...])
    la = dt * (-jnp.exp(alog_ref[...]))
    ti = lax.broadcasted_iota(jnp.int32, (CHUNK, CHUNK), 0)
    si = lax.broadcasted_iota(jnp.int32, (CHUNK, CHUNK), 1)
    lower = ti >= si
    upper = si >= ti
    lanes2h = lax.broadcasted_iota(jnp.int32, (CHUNK, 2 * nheads), 1)
    acs = jnp.where(lanes2h < nheads,
                    _cumsum_rows(jnp.where(lower, 1.0, 0.0).astype(BF16), la),
                    _cumsum_rows(jnp.where(upper, 1.0, 0.0).astype(BF16), la))
    col2 = acs * LOG2E
    row2_t = (jnp.log(dt) * LOG2E - col2).T
    acs_f, acs_b = acs[:, 0:nheads], acs[:, nheads:2 * nheads]
    e_f = _expand_heads(jnp.exp(acs_f), e3_ref)
    e_b = _expand_heads(jnp.exp(acs_b), e3_ref)
    w_f = dt[:, 0:nheads] * jnp.exp(acs_f[CHUNK - 1:CHUNK, :] - acs_f)
    x = x_ref[...].astype(F32)
    xdec = (x * _expand_heads(w_f, e3_ref)).astype(BF16)
    cdec = e_f[CHUNK - 1:CHUNK, :]

    hpg = nheads // SSD_GROUPS
    gw = hpg * SSD_HEADDIM
    lane = lax.broadcasted_iota(jnp.int32, (CHUNK, LANES), 1)
    left = (lane < SSD_HEADDIM)
    for g in range(SSD_GROUPS):
        cs = slice(g * gw, (g + 1) * gw)
        bg = b_ref[:, g * D_STATE:(g + 1) * D_STATE]
        cg = c_ref[:, g * D_STATE:(g + 1) * D_STATE]
        cb = lax.dot_general(cg, bg, (((1,), (1,)), ((), ())), preferred_element_type=F32)
        y_off = (jnp.dot(cg, state_ref[:, cs].astype(BF16), preferred_element_type=F32) * e_f[:, cs]
                 + jnp.dot(cg, prevb_ref[:, cs], preferred_element_type=F32) * e_b[:, cs])
        for pr in range(hpg // 2):
            mats = []
            for h in (g * hpg + 2 * pr, g * hpg + 2 * pr + 1):
                hb = nheads + h
                seg_f = col2[:, h:h + 1] + row2_t[h:h + 1, :]
                seg_b = col2[:, hb:hb + 1] + row2_t[hb:hb + 1, :]
                dec = jnp.exp2(jnp.where(lower, seg_f, NEG_BIG)) + jnp.exp2(jnp.where(upper, seg_b, NEG_BIG))
                mats.append((cb * dec).astype(BF16))
            ps = slice(g * gw + pr * LANES, g * gw + (pr + 1) * LANES)
            xp = x[:, ps]
            rhs = jnp.concatenate([jnp.where(left, xp, 0.0), jnp.where(left, 0.0, xp)], axis=0).astype(BF16)
            y_diag = jnp.dot(jnp.concatenate(mats, axis=1), rhs, preferred_element_type=F32)
            po = slice(pr * LANES, (pr + 1) * LANES)
            y_ref[:, ps] = y_diag + y_off[:, po] + dskip_ref[:, ps] * xp
        st = lax.dot_general(bg, xdec[:, cs], (((0,), (0,)), ((), ())), preferred_element_type=F32)
        state_ref[:, cs] = state_ref[:, cs] * cdec[:, cs] + st

    z = z_ref[...].astype(F32)
    u = y_ref[...] * (z * (1.0 / (1.0 + jnp.exp(-z))))
    for g in range(SSD_GROUPS):
        cs = slice(g * gw, (g + 1) * gw)
        ug = u[:, cs]
        ms = jnp.mean(ug * ug, axis=-1, keepdims=True)
        o_ref[:, cs] = (ug * lax.rsqrt(ms + LN_EPS) * nw_ref[:, cs]).astype(o_ref.dtype)


def _ssd_scan_norm(xbc, z, dt_raw, dtb, alog, dsk, nw, layer, *, layout, name):
    t, d_inner = z.shape
    nheads = d_inner // SSD_HEADDIM
    gn = SSD_GROUPS * D_STATE
    b_col, c_col = d_inner // gn, d_inner // gn + 1
    nchunks = t // CHUNK
    layout_chunks = tuple(v // CHUNK for v in layout)
    expand = np.repeat(np.eye(nheads, dtype=np.float32), SSD_HEADDIM, axis=1)
    e3 = jnp.asarray(np.concatenate([expand] * 3, axis=0), dtype=BF16)
    const = lambda c: (0, 0)
    par = lambda c: (layer, 0, 0)
    rev = lambda c: (nchunks - 1 - c, 0)
    fwd = lambda c: (c, 0)

    prev_b = pl.pallas_call(
        functools.partial(_ssd_bwd_kernel, nchunks=nchunks, layout_chunks=layout_chunks, nheads=nheads),
        grid=(nchunks,),
        in_specs=[pl.BlockSpec((CHUNK, d_inner), rev),
                  pl.BlockSpec((CHUNK, gn), lambda c: (nchunks - 1 - c, b_col)),
                  pl.BlockSpec((CHUNK, 2 * nheads), rev),
                  _resident((None, 1, 2 * nheads), par),
                  _resident((None, 1, 2 * nheads), par),
                  _resident((3 * nheads, d_inner), const)],
        out_specs=pl.BlockSpec((D_STATE, d_inner), rev),
        out_shape=jax.ShapeDtypeStruct((nchunks * D_STATE, d_inner), BF16),
        scratch_shapes=[pltpu.VMEM((D_STATE, d_inner), F32)],
        compiler_params=_cparams("arbitrary"),
        name=name + "_bwd",
    )(xbc, xbc, dt_raw, dtb, alog, e3)

    return pl.pallas_call(
        functools.partial(_ssd_fwd_kernel, layout_chunks=layout_chunks, nheads=nheads),
        grid=(nchunks,),
        in_specs=[pl.BlockSpec((CHUNK, d_inner), fwd),
                  pl.BlockSpec((CHUNK, d_inner), fwd),
                  pl.BlockSpec((CHUNK, gn), lambda c: (c, b_col)),
                  pl.BlockSpec((CHUNK, gn), lambda c: (c, c_col)),
                  pl.BlockSpec((CHUNK, 2 * nheads), fwd),
                  pl.BlockSpec((D_STATE, d_inner), fwd),
                  _resident((None, 1, 2 * nheads), par),
                  _resident((None, 1, 2 * nheads), par),
                  _resident((None, 1, d_inner), par),
                  _resident((None, 1, d_inner), par),
                  _resident((3 * nheads, d_inner), const)],
        out_specs=pl.BlockSpec((CHUNK, d_inner), fwd),
        out_shape=jax.ShapeDtypeStruct((t, d_inner), BF16),
        scratch_shapes=[pltpu.VMEM((D_STATE, d_inner), F32), pltpu.VMEM((CHUNK, d_inner), F32)],
        compiler_params=_cparams("arbitrary"),
        name=name + "_fwd",
    )(xbc, z, xbc, xbc, dt_raw, prev_b, dtb, alog, dsk, nw, e3)


def _sub_major_kernel(x_ref, *refs, rows, emit_natural):
    nd = (len(refs) - emit_natural) // 2
    bm, d = x_ref.shape
    for r0 in range(0, bm, rows):
        blk = x_ref[r0:r0 + rows, :].astype(BF16)
        if emit_natural:
            refs[-1][r0:r0 + rows, :] = blk
        for p_ref, o_ref in zip(refs[:nd], refs[nd:2 * nd]):
            dil = o_ref.shape[0]
            sub = jnp.dot(p_ref[...], blk, preferred_element_type=F32).astype(BF16)
            o_ref[:, r0 // dil:(r0 + rows) // dil, :] = sub.reshape(dil, rows // dil, d)


def _to_sub_major(xb, dils):
    t, d = xb.shape
    emit_natural = xb.dtype != BF16
    bm, rows = TILES["mix_bm"], TILES["out_rows"]
    perms = [jnp.asarray(_unpermute_matrix(rows, dil).T, dtype=BF16) for dil in dils]
    outs = pl.pallas_call(
        functools.partial(_sub_major_kernel, rows=rows, emit_natural=emit_natural),
        grid=(t // bm,),
        in_specs=[pl.BlockSpec((bm, d), lambda i: (i, 0))]
                 + [_resident((rows, rows), lambda i: (0, 0)) for _ in dils],
        out_specs=[pl.BlockSpec((dil, bm // dil, d), lambda i: (0, i, 0)) for dil in dils]
                  + [pl.BlockSpec((bm, d), lambda i: (i, 0))] * emit_natural,
        out_shape=[jax.ShapeDtypeStruct((dil, t // dil, d), BF16) for dil in dils]
                  + [jax.ShapeDtypeStruct((t, d), BF16)] * emit_natural,
        compiler_params=_cparams("parallel"),
        name="sub_major",
    )(xb, *perms)
    return [o.reshape(t, d) for o in outs]


def _alibi_slopes(n):
    return np.float32(2.0) ** (np.float32(-8.0) * np.arange(1, n + 1, dtype=np.float32) / np.float32(n))


def _attention_layer(x, xb, w_qkv, w_o, layer, g, b, *, alpha, layout):
    ngroups = len(DILATED_CONFIGS)
    hd_all = ATTN_HEADS * HEAD_DIM
    slopes = _alibi_slopes(ngroups * ATTN_HEADS).reshape(ngroups, ATTN_HEADS)
    outs, lses = [], []
    assert DILATED_CONFIGS[0][1] == 1
    subs = _to_sub_major(x if xb is None else xb, [dil for _, dil in DILATED_CONFIGS[1:]])
    xgs = [subs[-1] if xb is None else xb] + subs[:ngroups - 1]
    for gi, (window, dil) in enumerate(DILATED_CONFIGS):
        qkv = _matmul(xgs[gi], w_qkv, layer, gi * 3 * hd_all, 3 * hd_all, BF16, name=f"qkv_g{gi}")
        o, lse = _band_attention(qkv, dil=dil, window=window, slopes=slopes[gi], layout=layout, name=f"attn_g{gi}")
        outs.append(o)
        lses.append(lse)
    mix = _mix_groups(outs, lses, name="attn_mix")
    return _matmul_res_ln(mix, w_o, layer, x, g, b, alpha=alpha, name="attn_out")


def _ssd_layer(x, xb, w_in, conv_w, conv_b, dtb, alog, dsk, nw, w_out, layer, g, b, *, alpha, layout):
    d_inner = w_out.shape[1]
    nheads = d_inner // SSD_HEADDIM
    gn = SSD_GROUPS * D_STATE
    conv_dim = d_inner + 2 * gn
    z = _matmul(xb, w_in, layer, 0, d_inner, BF16, name="ssd_in_z")
    xbc = _matmul(xb, w_in, layer, d_inner, conv_dim, BF16, name="ssd_in_xbc")
    dt_raw = _matmul(xb, w_in, layer, d_inner + conv_dim, 2 * nheads, F32, name="ssd_in_dt")
    xbc = _conv_silu(xbc, conv_w, conv_b, layer, layout=layout, name="ssd_conv")
    y = _ssd_scan_norm(xbc, z, dt_raw, dtb, alog, dsk, nw, layer, layout=layout, name="ssd_scan")
    return _matmul_res_ln(y, w_out, layer, x, g, b, alpha=alpha, name="ssd_out")


def kernel(x_prompt, x_sample, attn_w_qkv, attn_w_o, ssd_w_in, ssd_conv_w, ssd_conv_b, ssd_dt_bias, ssd_a_log,
           ssd_d, ssd_norm_w, ssd_w_out, mlp_w1, mlp_w2, ln_g, ln_b):
    bp, sp, d = x_prompt.shape
    bs, ss, _ = x_sample.shape
    tp, ts = bp * sp, bs * ss
    depth = mlp_w1.shape[0]
    nssd, d_inner = ssd_w_out.shape[0], ssd_w_out.shape[1]
    nheads = d_inner // SSD_HEADDIM
    alpha = (2.0 * depth) ** 0.25
    layout = (tp, sp, ss)
    x = jnp.concatenate([x_prompt.reshape(tp, d), x_sample.reshape(ts, d)], axis=0)
    xb = None
    w_qkv, w_o = attn_w_qkv, attn_w_o.astype(BF16)
    w_in, w_out = ssd_w_in, ssd_w_out.astype(BF16)
    w1, w2 = mlp_w1, mlp_w2.astype(BF16)
    conv_b = ssd_conv_b.reshape(nssd, 1, -1)
    dtb = ssd_dt_bias.reshape(nssd, 1, 2 * nheads).astype(F32)
    alog = ssd_a_log.reshape(nssd, 1, 2 * nheads).astype(F32)
    dsk = jnp.repeat(ssd_d.astype(F32), SSD_HEADDIM, axis=1).reshape(nssd, 1, d_inner)
    nw = ssd_norm_w.reshape(nssd, 1, d_inner).astype(F32)
    for i in range(depth):
        j = i // 2
        g0, b0 = ln_g[i, 0].reshape(1, d), ln_b[i, 0].reshape(1, d)
        g1, b1 = ln_g[i, 1].reshape(1, d), ln_b[i, 1].reshape(1, d)
        if i % 2 == 0:
            x, xb = _attention_layer(x, xb, w_qkv, w_o, j, g0, b0, alpha=alpha, layout=layout)
        else:
            x, xb = _ssd_layer(x, xb, w_in, ssd_conv_w, conv_b, dtb, alog, dsk, nw, w_out, j, g0, b0,
                               alpha=alpha, layout=layout)
        hid = _matmul(xb, w1, i, 0, w1.shape[2], BF16, act="relu2", name="mlp_up")
        x, xb = _matmul_res_ln(hid, w2, i, x, g1, b1, alpha=alpha, name="mlp_down")
    return x[:tp].reshape(bp, sp, d), x[tp:].reshape(bs, ss, d)
```

```python
import functools
import math

import numpy as np
import jax
import jax.numpy as jnp
from jax import lax
from jax.experimental import pallas as pl
from jax.experimental.pallas import tpu as pltpu

F32 = jnp.float32
BF16 = jnp.bfloat16

DILATED_CONFIGS = ((128, 1), (512, 4), (2048, 16))
ATTN_HEADS = 16
HEAD_DIM = 128
SSD_HEADDIM = 64
SSD_GROUPS = 8
D_STATE = 128
D_CONV = 5
CHUNK = 128
LN_EPS = 1e-5
NEG_BIG = -1e30
LOG2E = 1.4426950408889634

V7X_VMEM_LIMIT_BYTES = 56 * 1024 * 1024
LANES = 128
SUBLANES = 8
BF16_ROWS = 16
CONV_ROWS = 128

TILES = dict(
    proj_bm=1024, proj_bn=1024,
    out_bm=1024, out_bk=1024,
    out_rows=256,
    mix_bm=512,
    attn_tq=256,
    attn_qb=128,
    conv_tr=512, conv_bc=1024,
)


def _cparams(*sem):
    return pltpu.CompilerParams(dimension_semantics=sem, vmem_limit_bytes=V7X_VMEM_LIMIT_BYTES)


def _resident(block_shape, index_map):
    return pl.BlockSpec(block_shape, index_map, pipeline_mode=pl.Buffered(1))


def _layer_norm_rows(y, g, b):
    mu = jnp.mean(y, axis=-1, keepdims=True)
    yc = y - mu
    var = jnp.mean(yc * yc, axis=-1, keepdims=True)
    return yc * lax.rsqrt(var + LN_EPS) * g + b


def _seq_bounds(row, layout):
    rows_p, seq_p, seq_s = layout
    in_p = row < rows_p
    lo_p = (row // seq_p) * seq_p
    lo_s = rows_p + ((row - rows_p) // seq_s) * seq_s
    lo = jnp.where(in_p, lo_p, lo_s)
    return lo, lo + jnp.where(in_p, seq_p, seq_s)


def _split3(x):
    hi = x.astype(BF16)
    r1 = x - hi.astype(F32)
    mid = r1.astype(BF16)
    lo = (r1 - mid.astype(F32)).astype(BF16)
    return hi, mid, lo


def _mm_kernel(x_ref, w_ref, o_ref, wb_ref, *, act):
    @pl.when(pl.program_id(1) == 0)
    def _():
        wb_ref[...] = w_ref[...].astype(BF16)

    acc = jnp.dot(x_ref[...], wb_ref[...], preferred_element_type=F32)
    if act == "relu2":
        acc = jnp.maximum(acc, 0.0)
        acc = acc * acc
    o_ref[...] = acc.astype(o_ref.dtype)


def _matmul(x, w, layer, col0, n, out_dtype, *, act=None, name):
    t, k = x.shape
    bm = TILES["proj_bm"]
    bn = min(TILES["proj_bn"], n)
    c0 = col0 // bn
    return pl.pallas_call(
        functools.partial(_mm_kernel, act=act),
        grid=(n // bn, t // bm),
        in_specs=[pl.BlockSpec((bm, k), lambda j, i: (i, 0)),
                  pl.BlockSpec((None, k, bn), lambda j, i: (layer, 0, c0 + j))],
        out_specs=pl.BlockSpec((bm, bn), lambda j, i: (i, j)),
        out_shape=jax.ShapeDtypeStruct((t, n), out_dtype),
        scratch_shapes=[pltpu.VMEM((k, bn), BF16)],
        compiler_params=_cparams("parallel", "arbitrary"),
        name=name,
    )(x, w)


def _mm_ln_kernel(l_ref, w_ref, r_ref, g_ref, b_ref, o_ref, ob_ref, *, nk, alpha, rows):
    kk = pl.program_id(1)

    if nk > 1:
        @pl.when(kk == 0)
        def _():
            o_ref[...] = jnp.dot(l_ref[...], w_ref[...], preferred_element_type=F32)

        @pl.when((kk > 0) & (kk < nk - 1))
        def _():
            o_ref[...] += jnp.dot(l_ref[...], w_ref[...], preferred_element_type=F32)

        wn = r_ref.shape[1]
        for c in range(nk):
            @pl.when(kk == c)
            def _(c=c):
                o_ref[:, c * wn:(c + 1) * wn] += alpha * r_ref[...]

    @pl.when(kk == nk - 1)
    def _():
        bm = l_ref.shape[0]
        for r0 in range(0, bm, rows):
            rs = slice(r0, r0 + rows)
            acc = jnp.dot(l_ref[rs, :], w_ref[...], preferred_element_type=F32)
            acc = acc + (o_ref[rs, :] if nk > 1 else alpha * r_ref[rs, :])
            y = _layer_norm_rows(acc, g_ref[...], b_ref[...])
            o_ref[rs, :] = y
            ob_ref[rs, :] = y.astype(BF16)


def _matmul_res_ln(lhs, w, layer, res, g, b, *, alpha, name):
    t, k = lhs.shape
    n = w.shape[2]
    bm, bk = TILES["out_bm"], TILES["out_bk"]
    nk = k // bk
    return pl.pallas_call(
        functools.partial(_mm_ln_kernel, nk=nk, alpha=alpha, rows=TILES["out_rows"]),
        grid=(t // bm, nk),
        in_specs=[pl.BlockSpec((bm, bk), lambda i, kk: (i, kk)),
                  pl.BlockSpec((None, bk, n), lambda i, kk: (layer, kk, 0)),
                  pl.BlockSpec((bm, n // nk), lambda i, kk: (i, kk)),
                  _resident((1, n), lambda i, kk: (0, 0)),
                  _resident((1, n), lambda i, kk: (0, 0))],
        out_specs=[pl.BlockSpec((bm, n), lambda i, kk: (i, 0)),
                   pl.BlockSpec((bm, n), lambda i, kk: (i, 0))],
        out_shape=[jax.ShapeDtypeStruct((t, n), F32), jax.ShapeDtypeStruct((t, n), BF16)],
        compiler_params=_cparams("parallel", "arbitrary"),
        name=name,
    )(lhs, w, res, g, b)


def _attn_kernel(q_ref, kp_ref, kc_ref, kn_ref, vp_ref, vc_ref, vn_ref, bias_ref, o_ref, lse_ref,
                 k_scr, v_scr, *, tq, half, layout):
    j = pl.program_id(1)
    row0 = j * tq
    lo, hi = _seq_bounds(row0, layout)
    wk = tq + 2 * half
    k_scr[0:half, :] = kp_ref[...]
    k_scr[half:half + tq, :] = kc_ref[...]
    k_scr[half + tq:wk, :] = kn_ref[...]
    v_scr[0:half, :] = vp_ref[...]
    v_scr[half:half + tq, :] = vc_ref[...]
    v_scr[half + tq:wk, :] = vn_ref[...]

    scale = HEAD_DIM ** -0.5
    qb = bias_ref.shape[1]
    wkb = qb + 2 * half
    lane = lax.broadcasted_iota(jnp.int32, (qb, LANES), 1)

    def attend(at_sequence_end):
        for q0 in range(0, tq, qb):
            qs, ks = slice(q0, q0 + qb), slice(q0, q0 + wkb)
            if at_sequence_end:
                kpos = row0 + q0 - half + lax.broadcasted_iota(jnp.int32, (1, wkb), 1)
                edge = jnp.where((kpos >= lo) & (kpos < hi), 0.0, NEG_BIG).astype(F32)
            lse_all = jnp.zeros((qb, LANES), F32)
            for h in range(ATTN_HEADS):
                cs = slice(h * HEAD_DIM, (h + 1) * HEAD_DIM)
                z = lax.dot_general(q_ref[qs, cs], k_scr[ks, cs], (((1,), (1,)), ((), ())),
                                    preferred_element_type=F32)
                z = z + bias_ref[h]
                if at_sequence_end:
                    z = z + edge
                m = jnp.max(z, axis=-1, keepdims=True)
                p = jnp.exp2((z - m) * (scale * LOG2E))
                den = jnp.sum(p, axis=-1, keepdims=True)
                pv = jnp.dot(p.astype(BF16), v_scr[ks, cs], preferred_element_type=F32)
                o_ref[qs, cs] = (pv * (1.0 / den)).astype(o_ref.dtype)
                lse_all = jnp.where(lane == h, scale * m + jnp.log(den), lse_all)
            lse_ref[qs, :] = lse_all

    touches_end = (row0 == lo) | (row0 + tq == hi)
    pl.when(touches_end)(functools.partial(attend, True))
    pl.when(jnp.logical_not(touches_end))(functools.partial(attend, False))


def _band_bias(qb, half, dil, slopes):
    wk = qb + 2 * half
    rel = np.arange(wk)[None, :] - half - np.arange(qb)[:, None]
    dist = (np.abs(rel) * dil).astype(np.float32)
    scale = np.float32(HEAD_DIM ** -0.5)
    bias = -(slopes.astype(np.float32)[:, None, None] * dist[None]) / scale
    return np.where((np.abs(rel) <= half)[None], bias, np.float32(NEG_BIG)).astype(np.float32)


def _band_attention(qkv, *, dil, window, slopes, layout, name):
    t = qkv.shape[0]
    hd_all = ATTN_HEADS * HEAD_DIM
    half = window // (2 * dil)
    tq = min(TILES["attn_tq"], min(layout[1:]) // dil)
    wk = tq + 2 * half
    rows = t // dil
    nt = rows // tq
    per = tq // half
    nhalf = t // half
    lay = tuple(v // dil for v in layout)
    bias = jnp.asarray(_band_bias(TILES["attn_qb"], half, dil, slopes))

    def cur(col):
        return pl.BlockSpec((tq, hd_all), lambda r, j: (r * nt + j, col))

    def prev(col):
        return pl.BlockSpec((half, hd_all), lambda r, j: (jnp.maximum((r * nt + j) * per - 1, 0), col))

    def nxt(col):
        return pl.BlockSpec((half, hd_all), lambda r, j: (jnp.minimum((r * nt + j + 1) * per, nhalf - 1), col))

    return pl.pallas_call(
        functools.partial(_attn_kernel, tq=tq, half=half, layout=lay),
        grid=(dil, nt),
        in_specs=[cur(0), prev(1), cur(1), nxt(1), prev(2), cur(2), nxt(2),
                  _resident(bias.shape, lambda r, j: (0, 0, 0))],
        out_specs=[pl.BlockSpec((tq, hd_all), lambda r, j: (r * nt + j, 0)),
                   pl.BlockSpec((tq, LANES), lambda r, j: (r * nt + j, 0))],
        out_shape=[jax.ShapeDtypeStruct((t, hd_all), BF16), jax.ShapeDtypeStruct((t, LANES), F32)],
        scratch_shapes=[pltpu.VMEM((wk, hd_all), BF16), pltpu.VMEM((wk, hd_all), BF16)],
        compiler_params=_cparams("parallel", "parallel"),
        name=name,
    )(qkv, qkv, qkv, qkv, qkv, qkv, qkv, bias)


def _mix_kernel(o0_ref, o1_ref, o2_ref, l0_ref, l1_ref, l2_ref, p1_ref, p2_ref, ex_ref, mix_ref, *, rows):
    bm, k = o0_ref.shape

    def unpermute(perm_ref, o_blk_ref, l_blk_ref, r0):
        dil = o_blk_ref.shape[0]
        run = slice(r0 // dil, (r0 + rows) // dil)
        o_nat = jnp.dot(perm_ref[...], o_blk_ref[:, run, :].reshape(rows, k), preferred_element_type=F32)
        l3 = jnp.dot(perm_ref[...], jnp.concatenate(_split3(l_blk_ref[:, run, :].reshape(rows, LANES)), axis=1),
                     preferred_element_type=F32)
        return o_nat, l3[:, 0:LANES] + l3[:, LANES:2 * LANES] + l3[:, 2 * LANES:3 * LANES]

    for r0 in range(0, bm, rows):
        rs = slice(r0, r0 + rows)
        o1, l1 = unpermute(p1_ref, o1_ref, l1_ref, r0)
        o2, l2 = unpermute(p2_ref, o2_ref, l2_ref, r0)
        l0 = l0_ref[rs, :]
        m = jnp.maximum(jnp.maximum(l0, l1), l2)
        e0, e1, e2 = jnp.exp(l0 - m), jnp.exp(l1 - m), jnp.exp(l2 - m)
        inv = 1.0 / (e0 + e1 + e2)

        def over_head_lanes(w):
            hi = w.astype(BF16)
            lo = (w - hi.astype(F32)).astype(BF16)
            return jnp.dot(jnp.concatenate([hi, lo], axis=1), ex_ref[...], preferred_element_type=F32)

        o0 = o0_ref[rs, :].astype(F32)
        mix = o0 + over_head_lanes(e1 * inv) * (o1 - o0) + over_head_lanes(e2 * inv) * (o2 - o0)
        mix_ref[rs, :] = mix.astype(BF16)


def _unpermute_matrix(bm, dil):
    p = np.zeros((bm, bm), np.float32)
    sub = np.arange(bm)
    r, i = sub // (bm // dil), sub % (bm // dil)
    p[i * dil + r, sub] = 1.0
    return p


def _mix_groups(outs, lses, *, name):
    t, k = outs[0].shape
    bm = TILES["mix_bm"]
    dils = [dil for _, dil in DILATED_CONFIGS]
    row = lambda i: (i, 0)
    const = lambda i: (0, 0)

    def sub_major(a, dil):
        return a.reshape(dil, t // dil, a.shape[1])

    def sub_spec(cols, dil):
        return pl.BlockSpec((dil, bm // dil, cols), lambda i: (0, i, 0))

    rows = TILES["out_rows"]
    perms = [jnp.asarray(_unpermute_matrix(rows, dil), dtype=BF16) for dil in dils[1:]]
    head_rows = np.zeros((LANES, k), np.float32)
    head_rows[:ATTN_HEADS] = np.repeat(np.eye(ATTN_HEADS, dtype=np.float32), HEAD_DIM, axis=1)
    expand = jnp.asarray(np.concatenate([head_rows, head_rows], axis=0), dtype=BF16)
    return pl.pallas_call(
        functools.partial(_mix_kernel, rows=rows),
        grid=(t // bm,),
        in_specs=[pl.BlockSpec((bm, k), row), sub_spec(k, dils[1]), sub_spec(k, dils[2]),
                  pl.BlockSpec((bm, LANES), row), sub_spec(LANES, dils[1]), sub_spec(LANES, dils[2]),
                  _resident((rows, rows), const), _resident((rows, rows), const),
                  _resident((2 * LANES, k), const)],
        out_specs=pl.BlockSpec((bm, k), row),
        out_shape=jax.ShapeDtypeStruct((t, k), BF16),
        compiler_params=_cparams("parallel"),
        name=name,
    )(outs[0], sub_major(outs[1], dils[1]), sub_major(outs[2], dils[2]),
      lses[0], sub_major(lses[1], dils[1]), sub_major(lses[2], dils[2]),
      perms[0], perms[1], expand)


def _conv_kernel(p_ref, c_ref, n_ref, s_ref, w_ref, b_ref, o_ref, ext_ref, *, tr, layout):
    i = pl.program_id(0)
    row0 = i * tr
    lo, hi = _seq_bounds(row0, layout)
    halo = BF16_ROWS
    zero = jnp.zeros_like(p_ref)
    ext_ref[0:halo, :] = jnp.where(row0 == lo, zero, p_ref[...])
    ext_ref[halo:halo + tr, :] = c_ref[...]
    ext_ref[halo + tr:tr + 2 * halo, :] = jnp.where(row0 + tr == hi, zero, n_ref[...])
    rb = CONV_ROWS
    mid = D_CONV // 2
    taps = [kk for kk in range(D_CONV) if kk != mid]
    for r0 in range(0, tr, rb):
        blk = ext_ref[r0:r0 + rb + 2 * halo, :]
        shifted = jnp.dot(s_ref[...], blk, preferred_element_type=F32)
        acc = b_ref[...] + w_ref[mid:mid + 1, :] * blk[halo:halo + rb, :].astype(F32)
        for j, kk in enumerate(taps):
            acc = acc + w_ref[kk:kk + 1, :] * shifted[j * rb:(j + 1) * rb, :]
        o_ref[r0:r0 + rb, :] = (acc * (1.0 / (1.0 + jnp.exp2(acc * (-LOG2E))))).astype(o_ref.dtype)


def _shift_matrix(rb, halo):
    offs = [kk - D_CONV // 2 for kk in range(D_CONV) if kk != D_CONV // 2]
    s = np.zeros((len(offs) * rb, rb + 2 * halo), np.float32)
    for j, off in enumerate(offs):
        s[j * rb + np.arange(rb), halo + np.arange(rb) + off] = 1.0
    return s


def _conv_silu(u, w, b, layer, *, layout, name):
    t, c = u.shape
    tr, bc = TILES["conv_tr"], TILES["conv_bc"]
    per = tr // BF16_ROWS
    nh = t // BF16_ROWS
    shift = jnp.asarray(_shift_matrix(CONV_ROWS, BF16_ROWS), dtype=BF16)
    return pl.pallas_call(
        functools.partial(_conv_kernel, tr=tr, layout=layout),
        grid=(t // tr, c // bc),
        in_specs=[pl.BlockSpec((BF16_ROWS, bc), lambda i, j: (jnp.maximum(i * per - 1, 0), j)),
                  pl.BlockSpec((tr, bc), lambda i, j: (i, j)),
                  pl.BlockSpec((BF16_ROWS, bc), lambda i, j: (jnp.minimum((i + 1) * per, nh - 1), j)),
                  _resident(shift.shape, lambda i, j: (0, 0)),
                  pl.BlockSpec((None, D_CONV, bc), lambda i, j: (layer, 0, j)),
                  pl.BlockSpec((None, 1, bc), lambda i, j: (layer, 0, j))],
        out_specs=pl.BlockSpec((tr, bc), lambda i, j: (i, j)),
        out_shape=jax.ShapeDtypeStruct((t, c), BF16),
        scratch_shapes=[pltpu.VMEM((tr + 2 * BF16_ROWS, bc), BF16)],
        compiler_params=_cparams("parallel", "parallel"),
        name=name,
    )(u, u, u, shift, w, b)


def _cumsum_rows(tri, la):
    out = jnp.dot(tri, jnp.concatenate(_split3(la), axis=1), preferred_element_type=F32)
    return out[:, 0:LANES] + out[:, LANES:2 * LANES] + out[:, 2 * LANES:3 * LANES]


def _expand_heads(q, e3_ref):
    return jnp.dot(jnp.concatenate(_split3(q), axis=1), e3_ref[...], preferred_element_type=F32)


def _softplus(x):
    return jnp.maximum(x, 0.0) + jnp.log(1.0 + jnp.exp(-jnp.abs(x)))


def _chunk_flags(c, layout_chunks):
    lo, hi = _seq_bounds(c, layout_chunks)
    return c == lo, c == hi - 1


def _ssd_bwd_kernel(x_ref, b_ref, dt_ref, dtb_ref, alog_ref, e3_ref, prev_ref, state_ref,
                    *, nchunks, layout_chunks, nheads):
    c = nchunks - 1 - pl.program_id(0)
    _, is_last = _chunk_flags(c, layout_chunks)

    @pl.when(is_last)
    def _():
        state_ref[...] = jnp.zeros_like(state_ref)

    prev_ref[...] = state_ref[...].astype(BF16)

    dt = _softplus(dt_ref[...] + dtb_ref[...])
    la = dt * (-jnp.exp(alog_ref[...]))
    ti = lax.broadcasted_iota(jnp.int32, (CHUNK, CHUNK), 0)
    si = lax.broadcasted_iota(jnp.int32, (CHUNK, CHUNK), 1)
    triu = jnp.where(si >= ti, 1.0, 0.0).astype(BF16)
    acs = _cumsum_rows(triu, la)[:, nheads:2 * nheads]
    dt_b = dt[:, nheads:2 * nheads]
    wgt = dt_b * jnp.exp(acs[0:1, :] - acs)
    xdec = (x_ref[...] * _expand_heads(wgt, e3_ref)).astype(BF16)
    cdec = _expand_heads(jnp.broadcast_to(jnp.exp(acs[0:1, :]), (SUBLANES, nheads)), e3_ref)[0:1, :]
    gw = (nheads // SSD_GROUPS) * SSD_HEADDIM
    for g in range(SSD_GROUPS):
        cs = slice(g * gw, (g + 1) * gw)
        bg = b_ref[:, g * D_STATE:(g + 1) * D_STATE]
        st = lax.dot_general(bg, xdec[:, cs], (((0,), (0,)), ((), ())), preferred_element_type=F32)
        state_ref[:, cs] = state_ref[:, cs] * cdec[:, cs] + st


def _ssd_fwd_kernel(x_ref, z_ref, b_ref, c_ref, dt_ref, prevb_ref, dtb_ref, alog_ref, dskip_ref,
                    nw_ref, e3_ref, o_ref, state_ref, y_ref, *, layout_chunks, nheads):
    c = pl.program_id(0)
    is_first, _ = _chunk_flags(c, layout_chunks)

    @pl.when(is_first)
    def _():
        state_ref[...] = jnp.zeros_like(state_ref)

    dt = _softplus(dt_ref[...] + dtb_ref[...])
    la = dt * (-jnp.exp(alog_ref[...]))
    ti = lax.broadcasted_iota(jnp.int32, (CHUNK, CHUNK), 0)
    si = lax.broadcasted_iota(jnp.int32, (CHUNK, CHUNK), 1)
    lower = ti >= si
    upper = si >= ti
    lanes2h = lax.broadcasted_iota(jnp.int32, (CHUNK, 2 * nheads), 1)
    acs = jnp.where(lanes2h < nheads,
                    _cumsum_rows(jnp.where(lower, 1.0, 0.0).astype(BF16), la),
                    _cumsum_rows(jnp.where(upper, 1.0, 0.0).astype(BF16), la))
    col2 = acs * LOG2E
    row2_t = (jnp.log(dt) * LOG2E - col2).T
    acs_f, acs_b = acs[:, 0:nheads], acs[:, nheads:2 * nheads]
    e_f = _expand_heads(jnp.exp(acs_f), e3_ref)
    e_b = _expand_heads(jnp.exp(acs_b), e3_ref)
    w_f = dt[:, 0:nheads] * jnp.exp(acs_f[CHUNK - 1:CHUNK, :] - acs_f)
    x = x_ref[...].astype(F32)
    xdec = (x * _expand_heads(w_f, e3_ref)).astype(BF16)
    cdec = e_f[CHUNK - 1:CHUNK, :]

    hpg = nheads // SSD_GROUPS
    gw = hpg * SSD_HEADDIM
    lane = lax.broadcasted_iota(jnp.int32, (CHUNK, LANES), 1)
    left = (lane < SSD_HEADDIM)
    for g in range(SSD_GROUPS):
        cs = slice(g * gw, (g + 1) * gw)
        bg = b_ref[:, g * D_STATE:(g + 1) * D_STATE]
        cg = c_ref[:, g * D_STATE:(g + 1) * D_STATE]
        cb = lax.dot_general(cg, bg, (((1,), (1,)), ((), ())), preferred_element_type=F32)
        y_off = (jnp.dot(cg, state_ref[:, cs].astype(BF16), preferred_element_type=F32) * e_f[:, cs]
                 + jnp.dot(cg, prevb_ref[:, cs], preferred_element_type=F32) * e_b[:, cs])
        for pr in range(hpg // 2):
            mats = []
            for h in (g * hpg + 2 * pr, g * hpg + 2 * pr + 1):
                hb = nheads + h
                seg_f = col2[:, h:h + 1] + row2_t[h:h + 1, :]
                seg_b = col2[:, hb:hb + 1] + row2_t[hb:hb + 1, :]
                dec = jnp.exp2(jnp.where(lower, seg_f, NEG_BIG)) + jnp.exp2(jnp.where(upper, seg_b, NEG_BIG))
                mats.append((cb * dec).astype(BF16))
            ps = slice(g * gw + pr * LANES, g * gw + (pr + 1) * LANES)
            xp = x[:, ps]
            rhs = jnp.concatenate([jnp.where(left, xp, 0.0), jnp.where(left, 0.0, xp)], axis=0).astype(BF16)
            y_diag = jnp.dot(jnp.concatenate(mats, axis=1), rhs, preferred_element_type=F32)
            po = slice(pr * LANES, (pr + 1) * LANES)
            y_ref[:, ps] = y_diag + y_off[:, po] + dskip_ref[:, ps] * xp
        st = lax.dot_general(bg, xdec[:, cs], (((0,), (0,)), ((), ())), preferred_element_type=F32)
        state_ref[:, cs] = state_ref[:, cs] * cdec[:, cs] + st

    z = z_ref[...].astype(F32)
    u = y_ref[...] * (z * (1.0 / (1.0 + jnp.exp(-z))))
    for g in range(SSD_GROUPS):
        cs = slice(g * gw, (g + 1) * gw)
        ug = u[:, cs]
        ms = jnp.mean(ug * ug, axis=-1, keepdims=True)
        o_ref[:, cs] = (ug * lax.rsqrt(ms + LN_EPS) * nw_ref[:, cs]).astype(o_ref.dtype)


def _ssd_scan_norm(xbc, z, dt_raw, dtb, alog, dsk, nw, layer, *, layout, name):
    t, d_inner = z.shape
    nheads = d_inner // SSD_HEADDIM
    gn = SSD_GROUPS * D_STATE
    b_col, c_col = d_inner // gn, d_inner // gn + 1
    nchunks = t // CHUNK
    layout_chunks = tuple(v // CHUNK for v in layout)
    expand = np.repeat(np.eye(nheads, dtype=np.float32), SSD_HEADDIM, axis=1)
    e3 = jnp.asarray(np.concatenate([expand] * 3, axis=0), dtype=BF16)
    const = lambda c: (0, 0)
    par = lambda c: (layer, 0, 0)
    rev = lambda c: (nchunks - 1 - c, 0)
    fwd = lambda c: (c, 0)

    prev_b = pl.pallas_call(
        functools.partial(_ssd_bwd_kernel, nchunks=nchunks, layout_chunks=layout_chunks, nheads=nheads),
        grid=(nchunks,),
        in_specs=[pl.BlockSpec((CHUNK, d_inner), rev),
                  pl.BlockSpec((CHUNK, gn), lambda c: (nchunks - 1 - c, b_col)),
                  pl.BlockSpec((CHUNK, 2 * nheads), rev),
                  _resident((None, 1, 2 * nheads), par),
                  _resident((None, 1, 2 * nheads), par),
                  _resident((3 * nheads, d_inner), const)],
        out_specs=pl.BlockSpec((D_STATE, d_inner), rev),
        out_shape=jax.ShapeDtypeStruct((nchunks * D_STATE, d_inner), BF16),
        scratch_shapes=[pltpu.VMEM((D_STATE, d_inner), F32)],
        compiler_params=_cparams("arbitrary"),
        name=name + "_bwd",
    )(xbc, xbc, dt_raw, dtb, alog, e3)

    return pl.pallas_call(
        functools.partial(_ssd_fwd_kernel, layout_chunks=layout_chunks, nheads=nheads),
        grid=(nchunks,),
        in_specs=[pl.BlockSpec((CHUNK, d_inner), fwd),
                  pl.BlockSpec((CHUNK, d_inner), fwd),
                  pl.BlockSpec((CHUNK, gn), lambda c: (c, b_col)),
                  pl.BlockSpec((CHUNK, gn), lambda c: (c, c_col)),
                  pl.BlockSpec((CHUNK, 2 * nheads), fwd),
                  pl.BlockSpec((D_STATE, d_inner), fwd),
                  _resident((None, 1, 2 * nheads), par),
                  _resident((None, 1, 2 * nheads), par),
                  _resident((None, 1, d_inner), par),
                  _resident((None, 1, d_inner), par),
                  _resident((3 * nheads, d_inner), const)],
        out_specs=pl.BlockSpec((CHUNK, d_inner), fwd),
        out_shape=jax.ShapeDtypeStruct((t, d_inner), BF16),
        scratch_shapes=[pltpu.VMEM((D_STATE, d_inner), F32), pltpu.VMEM((CHUNK, d_inner), F32)],
        compiler_params=_cparams("arbitrary"),
        name=name + "_fwd",
    )(xbc, z, xbc, xbc, dt_raw, prev_b, dtb, alog, dsk, nw, e3)


def _sub_major_kernel(x_ref, *refs, rows, emit_natural):
    nd = (len(refs) - emit_natural) // 2
    bm, d = x_ref.shape
    for r0 in range(0, bm, rows):
        blk = x_ref[r0:r0 + rows, :].astype(BF16)
        if emit_natural:
            refs[-1][r0:r0 + rows, :] = blk
        for p_ref, o_ref in zip(refs[:nd], refs[nd:2 * nd]):
            dil = o_ref.shape[0]
            sub = jnp.dot(p_ref[...], blk, preferred_element_type=F32).astype(BF16)
            o_ref[:, r0 // dil:(r0 + rows) // dil, :] = sub.reshape(dil, rows // dil, d)


def _to_sub_major(xb, dils):
    t, d = xb.shape
    emit_natural = xb.dtype != BF16
    bm, rows = TILES["mix_bm"], TILES["out_rows"]
    perms = [jnp.asarray(_unpermute_matrix(rows, dil).T, dtype=BF16) for dil in dils]
    outs = pl.pallas_call(
        functools.partial(_sub_major_kernel, rows=rows, emit_natural=emit_natural),
        grid=(t // bm,),
        in_specs=[pl.BlockSpec((bm, d), lambda i: (i, 0))]
                 + [_resident((rows, rows), lambda i: (0, 0)) for _ in dils],
        out_specs=[pl.BlockSpec((dil, bm // dil, d), lambda i: (0, i, 0)) for dil in dils]
                  + [pl.BlockSpec((bm, d), lambda i: (i, 0))] * emit_natural,
        out_shape=[jax.ShapeDtypeStruct((dil, t // dil, d), BF16) for dil in dils]
                  + [jax.ShapeDtypeStruct((t, d), BF16)] * emit_natural,
        compiler_params=_cparams("parallel"),
        name="sub_major",
    )(xb, *perms)
    return [o.reshape(t, d) for o in outs]


def _alibi_slopes(n):
    return np.float32(2.0) ** (np.float32(-8.0) * np.arange(1, n + 1, dtype=np.float32) / np.float32(n))


def _attention_layer(x, xb, w_qkv, w_o, layer, g, b, *, alpha, layout):
    ngroups = len(DILATED_CONFIGS)
    hd_all = ATTN_HEADS * HEAD_DIM
    slopes = _alibi_slopes(ngroups * ATTN_HEADS).reshape(ngroups, ATTN_HEADS)
    outs, lses = [], []
    assert DILATED_CONFIGS[0][1] == 1
    subs = _to_sub_major(x if xb is None else xb, [dil for _, dil in DILATED_CONFIGS[1:]])
    xgs = [subs[-1] if xb is None else xb] + subs[:ngroups - 1]
    for gi, (window, dil) in enumerate(DILATED_CONFIGS):
        qkv = _matmul(xgs[gi], w_qkv, layer, gi * 3 * hd_all, 3 * hd_all, BF16, name=f"qkv_g{gi}")
        o, lse = _band_attention(qkv, dil=dil, window=window, slopes=slopes[gi], layout=layout, name=f"attn_g{gi}")
        outs.append(o)
        lses.append(lse)
    mix = _mix_groups(outs, lses, name="attn_mix")
    return _matmul_res_ln(mix, w_o, layer, x, g, b, alpha=alpha, name="attn_out")


def _ssd_layer(x, xb, w_in, conv_w, conv_b, dtb, alog, dsk, nw, w_out, layer, g, b, *, alpha, layout):
    d_inner = w_out.shape[1]
    nheads = d_inner // SSD_HEADDIM
    gn = SSD_GROUPS * D_STATE
    conv_dim = d_inner + 2 * gn
    z = _matmul(xb, w_in, layer, 0, d_inner, BF16, name="ssd_in_z")
    xbc = _matmul(xb, w_in, layer, d_inner, conv_dim, BF16, name="ssd_in_xbc")
    dt_raw = _matmul(xb, w_in, layer, d_inner + conv_dim, 2 * nheads, F32, name="ssd_in_dt")
    xbc = _conv_silu(xbc, conv_w, conv_b, layer, layout=layout, name="ssd_conv")
    y = _ssd_scan_norm(xbc, z, dt_raw, dtb, alog, dsk, nw, layer, layout=layout, name="ssd_scan")
    return _matmul_res_ln(y, w_out, layer, x, g, b, alpha=alpha, name="ssd_out")


def kernel(x_prompt, x_sample, attn_w_qkv, attn_w_o, ssd_w_in, ssd_conv_w, ssd_conv_b, ssd_dt_bias, ssd_a_log,
           ssd_d, ssd_norm_w, ssd_w_out, mlp_w1, mlp_w2, ln_g, ln_b):
    bp, sp, d = x_prompt.shape
    bs, ss, _ = x_sample.shape
    tp, ts = bp * sp, bs * ss
    depth = mlp_w1.shape[0]
    nssd, d_inner = ssd_w_out.shape[0], ssd_w_out.shape[1]
    nheads = d_inner // SSD_HEADDIM
    alpha = (2.0 * depth) ** 0.25
    layout = (tp, sp, ss)
    x = jnp.concatenate([x_prompt.reshape(tp, d), x_sample.reshape(ts, d)], axis=0)
    xb = None
    w_qkv, w_o = attn_w_qkv, attn_w_o.astype(BF16)
    w_in, w_out = ssd_w_in, ssd_w_out.astype(BF16)
    w1, w2 = mlp_w1, mlp_w2.astype(BF16)
    conv_b = ssd_conv_b.reshape(nssd, 1, -1)
    dtb = ssd_dt_bias.reshape(nssd, 1, 2 * nheads).astype(F32)
    alog = ssd_a_log.reshape(nssd, 1, 2 * nheads).astype(F32)
    dsk = jnp.repeat(ssd_d.astype(F32), SSD_HEADDIM, axis=1).reshape(nssd, 1, d_inner)
    nw = ssd_norm_w.reshape(nssd, 1, d_inner).astype(F32)
    for i in range(depth):
        j = i // 2
        g0, b0 = ln_g[i, 0].reshape(1, d), ln_b[i, 0].reshape(1, d)
        g1, b1 = ln_g[i, 1].reshape(1, d), ln_b[i, 1].reshape(1, d)
        if i % 2 == 0:
            x, xb = _attention_layer(x, xb, w_qkv, w_o, j, g0, b0, alpha=alpha, layout=layout)
        else:
            x, xb = _ssd_layer(x, xb, w_in, ssd_conv_w, conv_b, dtb, alog, dsk, nw, w_out, j, g0, b0,
                               alpha=alpha, layout=layout)
        hid = _matmul(xb, w1, i, 0, w1.shape[2], BF16, act="relu2", name="mlp_up")
        x, xb = _matmul_res_ln(hid, w2, i, x, g1, b1, alpha=alpha, name="mlp_down")
    return x[:tp].reshape(bp, sp, d), x[tp:].reshape(bs, ss, d)
```

```python
import functools
import math

import numpy as np
import jax
import jax.numpy as jnp
from jax import lax
from jax.experimental import pallas as pl
from jax.experimental.pallas import tpu as pltpu

F32 = jnp.float32
BF16 = jnp.bfloat16

DILATED_CONFIGS = ((128, 1), (512, 4), (2048, 16))
ATTN_HEADS = 16
HEAD_DIM = 128
SSD_HEADDIM = 64
SSD_GROUPS = 8
D_STATE = 128
D_CONV = 5
CHUNK = 128
LN_EPS = 1e-5
NEG_BIG = -1e30
LOG2E = 1.4426950408889634

V7X_VMEM_LIMIT_BYTES = 56 * 1024 * 1024
LANES = 128
SUBLANES = 8
BF16_ROWS = 16
CONV_ROWS = 128

TILES = dict(
    proj_bm=1024, proj_bn=1024,
    out_bm=1024, out_bk=1024,
    out_rows=256,
    wo_bm=512,
    mix_bm=512,
    attn_tq=512,
    attn_qb=128,
    conv_tr=512, conv_bc=1024,
)


def _cparams(*sem):
    return pltpu.CompilerParams(dimension_semantics=sem, vmem_limit_bytes=V7X_VMEM_LIMIT_BYTES)


def _resident(block_shape, index_map):
    return pl.BlockSpec(block_shape, index_map, pipeline_mode=pl.Buffered(1))


def _layer_norm_rows(y, g, b):
    mu = jnp.mean(y, axis=-1, keepdims=True)
    yc = y - mu
    var = jnp.mean(yc * yc, axis=-1, keepdims=True)
    return yc * lax.rsqrt(var + LN_EPS) * g + b


def _seq_bounds(row, layout):
    rows_p, seq_p, seq_s = layout
    in_p = row < rows_p
    lo_p = (row // seq_p) * seq_p
    lo_s = rows_p + ((row - rows_p) // seq_s) * seq_s
    lo = jnp.where(in_p, lo_p, lo_s)
    return lo, lo + jnp.where(in_p, seq_p, seq_s)


def _split3(x):
    hi = x.astype(BF16)
    r1 = x - hi.astype(F32)
    mid = r1.astype(BF16)
    lo = (r1 - mid.astype(F32)).astype(BF16)
    return hi, mid, lo


def _mm_kernel(x_ref, w_ref, o_ref, wb_ref, *, act):
    @pl.when(pl.program_id(1) == 0)
    def _():
        wb_ref[...] = w_ref[...].astype(BF16)

    acc = jnp.dot(x_ref[...], wb_ref[...], preferred_element_type=F32)
    if act == "relu2":
        acc = jnp.maximum(acc, 0.0)
        acc = acc * acc
    o_ref[...] = acc.astype(o_ref.dtype)


def _matmul(x, w, layer, col0, n, out_dtype, *, act=None, name):
    t, k = x.shape
    bm = TILES["proj_bm"]
    bn = min(TILES["proj_bn"], n)
    c0 = col0 // bn
    return pl.pallas_call(
        functools.partial(_mm_kernel, act=act),
        grid=(n // bn, t // bm),
        in_specs=[pl.BlockSpec((bm, k), lambda j, i: (i, 0)),
                  pl.BlockSpec((None, k, bn), lambda j, i: (layer, 0, c0 + j))],
        out_specs=pl.BlockSpec((bm, bn), lambda j, i: (i, j)),
        out_shape=jax.ShapeDtypeStruct((t, n), out_dtype),
        scratch_shapes=[pltpu.VMEM((k, bn), BF16)],
        compiler_params=_cparams("parallel", "arbitrary"),
        name=name,
    )(x, w)


def _mm_ln_kernel(l_ref, w_ref, r_ref, g_ref, b_ref, o_ref, ob_ref, *, nk, alpha, rows):
    kk = pl.program_id(1)

    if nk > 1:
        @pl.when(kk == 0)
        def _():
            o_ref[...] = jnp.dot(l_ref[...], w_ref[...], preferred_element_type=F32)

        @pl.when((kk > 0) & (kk < nk - 1))
        def _():
            o_ref[...] += jnp.dot(l_ref[...], w_ref[...], preferred_element_type=F32)

        wn = r_ref.shape[1]
        for c in range(nk):
            @pl.when(kk == c)
            def _(c=c):
                o_ref[:, c * wn:(c + 1) * wn] += alpha * r_ref[...]

    @pl.when(kk == nk - 1)
    def _():
        bm = l_ref.shape[0]
        for r0 in range(0, bm, rows):
            rs = slice(r0, r0 + rows)
            acc = jnp.dot(l_ref[rs, :], w_ref[...], preferred_element_type=F32)
            acc = acc + (o_ref[rs, :] if nk > 1 else alpha * r_ref[rs, :])
            y = _layer_norm_rows(acc, g_ref[...], b_ref[...])
            o_ref[rs, :] = y
            ob_ref[rs, :] = y.astype(BF16)


def _matmul_res_ln(lhs, w, layer, res, g, b, *, alpha, name, whole_k=False):
    t, k = lhs.shape
    n = w.shape[2]
    bm, bk = (TILES["wo_bm"], k) if whole_k else (TILES["out_bm"], TILES["out_bk"])
    nk = k // bk
    w_spec = (_resident if nk == 1 else pl.BlockSpec)((None, bk, n), lambda i, kk: (layer, kk, 0))
    return pl.pallas_call(
        functools.partial(_mm_ln_kernel, nk=nk, alpha=alpha, rows=TILES["out_rows"]),
        grid=(t // bm, nk),
        in_specs=[pl.BlockSpec((bm, bk), lambda i, kk: (i, kk)),
                  w_spec,
                  pl.BlockSpec((bm, n // nk), lambda i, kk: (i, kk)),
                  _resident((1, n), lambda i, kk: (0, 0)),
                  _resident((1, n), lambda i, kk: (0, 0))],
        out_specs=[pl.BlockSpec((bm, n), lambda i, kk: (i, 0)),
                   pl.BlockSpec((bm, n), lambda i, kk: (i, 0))],
        out_shape=[jax.ShapeDtypeStruct((t, n), F32), jax.ShapeDtypeStruct((t, n), BF16)],
        compiler_params=_cparams("parallel", "arbitrary"),
        name=name,
    )(lhs, w, res, g, b)


def _attn_kernel(q_ref, kp_ref, kc_ref, kn_ref, vp_ref, vc_ref, vn_ref, bias_ref, o_ref, lse_ref,
                 k_scr, v_scr, *, tq, half, layout):
    j = pl.program_id(1)
    row0 = j * tq
    lo, hi = _seq_bounds(row0, layout)
    wk = tq + 2 * half
    k_scr[0:half, :] = kp_ref[...]
    k_scr[half:half + tq, :] = kc_ref[...]
    k_scr[half + tq:wk, :] = kn_ref[...]
    v_scr[0:half, :] = vp_ref[...]
    v_scr[half:half + tq, :] = vc_ref[...]
    v_scr[half + tq:wk, :] = vn_ref[...]

    scale = HEAD_DIM ** -0.5
    qb = bias_ref.shape[1]
    wkb = qb + 2 * half
    lane = lax.broadcasted_iota(jnp.int32, (qb, LANES), 1)

    def attend(at_sequence_end):
        for q0 in range(0, tq, qb):
            qs, ks = slice(q0, q0 + qb), slice(q0, q0 + wkb)
            if at_sequence_end:
                kpos = row0 + q0 - half + lax.broadcasted_iota(jnp.int32, (1, wkb), 1)
                edge = jnp.where((kpos >= lo) & (kpos < hi), 0.0, NEG_BIG).astype(F32)
            lse_all = jnp.zeros((qb, LANES), F32)
            for h in range(ATTN_HEADS):
                cs = slice(h * HEAD_DIM, (h + 1) * HEAD_DIM)
                z = lax.dot_general(q_ref[qs, cs], k_scr[ks, cs], (((1,), (1,)), ((), ())),
                                    preferred_element_type=F32)
                z = z + bias_ref[h]
                if at_sequence_end:
                    z = z + edge
                m = jnp.max(z, axis=-1, keepdims=True)
                p = jnp.exp2((z - m) * (scale * LOG2E))
                den = jnp.sum(p, axis=-1, keepdims=True)
                pv = jnp.dot(p.astype(BF16), v_scr[ks, cs], preferred_element_type=F32)
                o_ref[qs, cs] = (pv * (1.0 / den)).astype(o_ref.dtype)
                lse_all = jnp.where(lane == h, scale * m + jnp.log(den), lse_all)
            lse_ref[qs, :] = lse_all

    touches_end = (row0 == lo) | (row0 + tq == hi)
    pl.when(touches_end)(functools.partial(attend, True))
    pl.when(jnp.logical_not(touches_end))(functools.partial(attend, False))


def _band_bias(qb, half, dil, slopes):
    wk = qb + 2 * half
    rel = np.arange(wk)[None, :] - half - np.arange(qb)[:, None]
    dist = (np.abs(rel) * dil).astype(np.float32)
    scale = np.float32(HEAD_DIM ** -0.5)
    bias = -(slopes.astype(np.float32)[:, None, None] * dist[None]) / scale
    return np.where((np.abs(rel) <= half)[None], bias, np.float32(NEG_BIG)).astype(np.float32)


def _band_attention(qkv, *, dil, window, slopes, layout, name):
    t = qkv.shape[0]
    hd_all = ATTN_HEADS * HEAD_DIM
    half = window // (2 * dil)
    tq = min(TILES["attn_tq"], min(layout[1:]) // dil)
    wk = tq + 2 * half
    rows = t // dil
    nt = rows // tq
    per = tq // half
    nhalf = t // half
    lay = tuple(v // dil for v in layout)
    bias = jnp.asarray(_band_bias(TILES["attn_qb"], half, dil, slopes))

    def cur(col):
        return pl.BlockSpec((tq, hd_all), lambda r, j: (r * nt + j, col))

    def prev(col):
        return pl.BlockSpec((half, hd_all), lambda r, j: (jnp.maximum((r * nt + j) * per - 1, 0), col))

    def nxt(col):
        return pl.BlockSpec((half, hd_all), lambda r, j: (jnp.minimum((r * nt + j + 1) * per, nhalf - 1), col))

    return pl.pallas_call(
        functools.partial(_attn_kernel, tq=tq, half=half, layout=lay),
        grid=(dil, nt),
        in_specs=[cur(0), prev(1), cur(1), nxt(1), prev(2), cur(2), nxt(2),
                  _resident(bias.shape, lambda r, j: (0, 0, 0))],
        out_specs=[pl.BlockSpec((tq, hd_all), lambda r, j: (r * nt + j, 0)),
                   pl.BlockSpec((tq, LANES), lambda r, j: (r * nt + j, 0))],
        out_shape=[jax.ShapeDtypeStruct((t, hd_all), BF16), jax.ShapeDtypeStruct((t, LANES), F32)],
        scratch_shapes=[pltpu.VMEM((wk, hd_all), BF16), pltpu.VMEM((wk, hd_all), BF16)],
        compiler_params=_cparams("parallel", "parallel"),
        name=name,
    )(qkv, qkv, qkv, qkv, qkv, qkv, qkv, bias)


def _mix_kernel(o0_ref, o1_ref, o2_ref, l0_ref, l1_ref, l2_ref, p1_ref, p2_ref, ex_ref, mix_ref, *, rows):
    bm, k = o0_ref.shape

    def unpermute(perm_ref, o_blk_ref, l_blk_ref, r0):
        dil = o_blk_ref.shape[0]
        run = slice(r0 // dil, (r0 + rows) // dil)
        o_nat = jnp.dot(perm_ref[...], o_blk_ref[:, run, :].reshape(rows, k), preferred_element_type=F32)
        l3 = jnp.dot(perm_ref[...], jnp.concatenate(_split3(l_blk_ref[:, run, :].reshape(rows, LANES)), axis=1),
                     preferred_element_type=F32)
        return o_nat, l3[:, 0:LANES] + l3[:, LANES:2 * LANES] + l3[:, 2 * LANES:3 * LANES]

    for r0 in range(0, bm, rows):
        rs = slice(r0, r0 + rows)
        o1, l1 = unpermute(p1_ref, o1_ref, l1_ref, r0)
        o2, l2 = unpermute(p2_ref, o2_ref, l2_ref, r0)
        l0 = l0_ref[rs, :]
        m = jnp.maximum(jnp.maximum(l0, l1), l2)
        e0, e1, e2 = jnp.exp(l0 - m), jnp.exp(l1 - m), jnp.exp(l2 - m)
        inv = 1.0 / (e0 + e1 + e2)

        def over_head_lanes(w):
            hi = w.astype(BF16)
            lo = (w - hi.astype(F32)).astype(BF16)
            return jnp.dot(jnp.concatenate([hi, lo], axis=1), ex_ref[...], preferred_element_type=F32)

        o0 = o0_ref[rs, :].astype(F32)
        mix = o0 + over_head_lanes(e1 * inv) * (o1 - o0) + over_head_lanes(e2 * inv) * (o2 - o0)
        mix_ref[rs, :] = mix.astype(BF16)


def _unpermute_matrix(bm, dil):
    p = np.zeros((bm, bm), np.float32)
    sub = np.arange(bm)
    r, i = sub // (bm // dil), sub % (bm // dil)
    p[i * dil + r, sub] = 1.0
    return p


def _mix_groups(outs, lses, *, name):
    t, k = outs[0].shape
    bm = TILES["mix_bm"]
    dils = [dil for _, dil in DILATED_CONFIGS]
    row = lambda i: (i, 0)
    const = lambda i: (0, 0)

    def sub_major(a, dil):
        return a.reshape(dil, t // dil, a.shape[1])

    def sub_spec(cols, dil):
        return pl.BlockSpec((dil, bm // dil, cols), lambda i: (0, i, 0))

    rows = TILES["out_rows"]
    perms = [jnp.asarray(_unpermute_matrix(rows, dil), dtype=BF16) for dil in dils[1:]]
    head_rows = np.zeros((LANES, k), np.float32)
    head_rows[:ATTN_HEADS] = np.repeat(np.eye(ATTN_HEADS, dtype=np.float32), HEAD_DIM, axis=1)
    expand = jnp.asarray(np.concatenate([head_rows, head_rows], axis=0), dtype=BF16)
    return pl.pallas_call(
        functools.partial(_mix_kernel, rows=rows),
        grid=(t // bm,),
        in_specs=[pl.BlockSpec((bm, k), row), sub_spec(k, dils[1]), sub_spec(k, dils[2]),
                  pl.BlockSpec((bm, LANES), row), sub_spec(LANES, dils[1]), sub_spec(LANES, dils[2]),
                  _resident((rows, rows), const), _resident((rows, rows), const),
                  _resident((2 * LANES, k), const)],
        out_specs=pl.BlockSpec((bm, k), row),
        out_shape=jax.ShapeDtypeStruct((t, k), BF16),
        compiler_params=_cparams("parallel"),
        name=name,
    )(outs[0], sub_major(outs[1], dils[1]), sub_major(outs[2], dils[2]),
      lses[0], sub_major(lses[1], dils[1]), sub_major(lses[2], dils[2]),
      perms[0], perms[1], expand)


def _conv_kernel(p_ref, c_ref, n_ref, s_ref, w_ref, b_ref, o_ref, ext_ref, *, tr, layout):
    i = pl.program_id(0)
    row0 = i * tr
    lo, hi = _seq_bounds(row0, layout)
    halo = BF16_ROWS
    zero = jnp.zeros_like(p_ref)
    ext_ref[0:halo, :] = jnp.where(row0 == lo, zero, p_ref[...])
    ext_ref[halo:halo + tr, :] = c_ref[...]
    ext_ref[halo + tr:tr + 2 * halo, :] = jnp.where(row0 + tr == hi, zero, n_ref[...])
    rb = CONV_ROWS
    mid = D_CONV // 2
    taps = [kk for kk in range(D_CONV) if kk != mid]
    for r0 in range(0, tr, rb):
        blk = ext_ref[r0:r0 + rb + 2 * halo, :]
        shifted = jnp.dot(s_ref[...], blk, preferred_element_type=F32)
        acc = b_ref[...] + w_ref[mid:mid + 1, :] * blk[halo:halo + rb, :].astype(F32)
        for j, kk in enumerate(taps):
            acc = acc + w_ref[kk:kk + 1, :] * shifted[j * rb:(j + 1) * rb, :]
        o_ref[r0:r0 + rb, :] = (acc * (1.0 / (1.0 + jnp.exp2(acc * (-LOG2E))))).astype(o_ref.dtype)


def _shift_matrix(rb, halo):
    offs = [kk - D_CONV // 2 for kk in range(D_CONV) if kk != D_CONV // 2]
    s = np.zeros((len(offs) * rb, rb + 2 * halo), np.float32)
    for j, off in enumerate(offs):
        s[j * rb + np.arange(rb), halo + np.arange(rb) + off] = 1.0
    return s


def _conv_silu(u, w, b, layer, *, layout, name):
    t, c = u.shape
    tr, bc = TILES["conv_tr"], TILES["conv_bc"]
    per = tr // BF16_ROWS
    nh = t // BF16_ROWS
    shift = jnp.asarray(_shift_matrix(CONV_ROWS, BF16_ROWS), dtype=BF16)
    return pl.pallas_call(
        functools.partial(_conv_kernel, tr=tr, layout=layout),
        grid=(t // tr, c // bc),
        in_specs=[pl.BlockSpec((BF16_ROWS, bc), lambda i, j: (jnp.maximum(i * per - 1, 0), j)),
                  pl.BlockSpec((tr, bc), lambda i, j: (i, j)),
                  pl.BlockSpec((BF16_ROWS, bc), lambda i, j: (jnp.minimum((i + 1) * per, nh - 1), j)),
                  _resident(shift.shape, lambda i, j: (0, 0)),
                  pl.BlockSpec((None, D_CONV, bc), lambda i, j: (layer, 0, j)),
                  pl.BlockSpec((None, 1, bc), lambda i, j: (layer, 0, j))],
        out_specs=pl.BlockSpec((tr, bc), lambda i, j: (i, j)),
        out_shape=jax.ShapeDtypeStruct((t, c), BF16),
        scratch_shapes=[pltpu.VMEM((tr + 2 * BF16_ROWS, bc), BF16)],
        compiler_params=_cparams("parallel", "parallel"),
        name=name,
    )(u, u, u, shift, w, b)


def _cumsum_rows(tri, la):
    out = jnp.dot(tri, jnp.concatenate(_split3(la), axis=1), preferred_element_type=F32)
    return out[:, 0:LANES] + out[:, LANES:2 * LANES] + out[:, 2 * LANES:3 * LANES]


def _expand_heads(q, e3_ref):
    return jnp.dot(jnp.concatenate(_split3(q), axis=1), e3_ref[...], preferred_element_type=F32)


def _softplus(x):
    return jnp.maximum(x, 0.0) + jnp.log(1.0 + jnp.exp(-jnp.abs(x)))


def _chunk_flags(c, layout_chunks):
    lo, hi = _seq_bounds(c, layout_chunks)
    return c == lo, c == hi - 1


def _ssd_bwd_kernel(x_ref, b_ref, dt_ref, dtb_ref, alog_ref, e3_ref, prev_ref, state_ref,
                    *, nchunks, layout_chunks, nheads):
    c = nchunks - 1 - pl.program_id(0)
    _, is_last = _chunk_flags(c, layout_chunks)

    @pl.when(is_last)
    def _():
        state_ref[...] = jnp.zeros_like(state_ref)

    prev_ref[...] = state_ref[...].astype(BF16)

    dt = _softplus(dt_ref[...] + dtb_ref[...])
    la = dt * (-jnp.exp(alog_ref[...]))
    ti = lax.broadcasted_iota(jnp.int32, (CHUNK, CHUNK), 0)
    si = lax.broadcasted_iota(jnp.int32, (CHUNK, CHUNK), 1)
    triu = jnp.where(si >= ti, 1.0, 0.0).astype(BF16)
    acs = _cumsum_rows(triu, la)[:, nheads:2 * nheads]
    dt_b = dt[:, nheads:2 * nheads]
    wgt = dt_b * jnp.exp(acs[0:1, :] - acs)
    xdec = (x_ref[...] * _expand_heads(wgt, e3_ref)).astype(BF16)
    cdec = _expand_heads(jnp.broadcast_to(jnp.exp(acs[0:1, :]), (SUBLANES, nheads)), e3_ref)[0:1, :]
    gw = (nheads // SSD_GROUPS) * SSD_HEADDIM
    for g in range(SSD_GROUPS):
        cs = slice(g * gw, (g + 1) * gw)
        bg = b_ref[:, g * D_STATE:(g + 1) * D_STATE]
        st = lax.dot_general(bg, xdec[:, cs], (((0,), (0,)), ((), ())), preferred_element_type=F32)
        state_ref[:, cs] = state_ref[:, cs] * cdec[:, cs] + st


def _ssd_fwd_kernel(x_ref, z_ref, b_ref, c_ref, dt_ref, prevb_ref, dtb_ref, alog_ref, dskip_ref,
                    nw_ref, e3_ref, o_ref, state_ref, y_ref, *, layout_chunks, nheads):
    c = pl.program_id(0)
    is_first, _ = _chunk_flags(c, layout_chunks)

    @pl.when(is_first)
    def _():
        state_ref[...] = jnp.zeros_like(state_ref)

    dt = _softplus(dt_ref[...] + dtb_ref[...])
    la = dt * (-jnp.exp(alog_ref[...]))
    ti = lax.broadcasted_iota(jnp.int32, (CHUNK, CHUNK), 0)
    si = lax.broadcasted_iota(jnp.int32, (CHUNK, CHUNK), 1)
    lower = ti >= si
    upper = si >= ti
    lanes2h = lax.broadcasted_iota(jnp.int32, (CHUNK, 2 * nheads), 1)
    acs = jnp.where(lanes2h < nheads,
                    _cumsum_rows(jnp.where(lower, 1.0, 0.0).astype(BF16), la),
                    _cumsum_rows(jnp.where(upper, 1.0, 0.0).astype(BF16), la))
    col2 = acs * LOG2E
    row2_t = (jnp.log(dt) * LOG2E - col2).T
    acs_f, acs_b = acs[:, 0:nheads], acs[:, nheads:2 * nheads]
    e_f = _expand_heads(jnp.exp(acs_f), e3_ref)
    e_b = _expand_heads(jnp.exp(acs_b), e3_ref)
    w_f = dt[:, 0:nheads] * jnp.exp(acs_f[CHUNK - 1:CHUNK, :] - acs_f)
    x = x_ref[...].astype(F32)
    xdec = (x * _expand_heads(w_f, e3_ref)).astype(BF16)
    cdec = e_f[CHUNK - 1:CHUNK, :]

    hpg = nheads // SSD_GROUPS
    gw = hpg * SSD_HEADDIM
    lane = lax.broadcasted_iota(jnp.int32, (CHUNK, LANES), 1)
    left = (lane < SSD_HEADDIM)
    for g in range(SSD_GROUPS):
        cs = slice(g * gw, (g + 1) * gw)
        bg = b_ref[:, g * D_STATE:(g + 1) * D_STATE]
        cg = c_ref[:, g * D_STATE:(g + 1) * D_STATE]
        cb = lax.dot_general(cg, bg, (((1,), (1,)), ((), ())), preferred_element_type=F32)
        y_off = (jnp.dot(cg, state_ref[:, cs].astype(BF16), preferred_element_type=F32) * e_f[:, cs]
                 + jnp.dot(cg, prevb_ref[:, cs], preferred_element_type=F32) * e_b[:, cs])
        for pr in range(hpg // 2):
            mats = []
            for h in (g * hpg + 2 * pr, g * hpg + 2 * pr + 1):
                hb = nheads + h
                seg_f = col2[:, h:h + 1] + row2_t[h:h + 1, :]
                seg_b = col2[:, hb:hb + 1] + row2_t[hb:hb + 1, :]
                dec = jnp.exp2(jnp.where(lower, seg_f, NEG_BIG)) + jnp.exp2(jnp.where(upper, seg_b, NEG_BIG))
                mats.append((cb * dec).astype(BF16))
            ps = slice(g * gw + pr * LANES, g * gw + (pr + 1) * LANES)
            xp = x[:, ps]
            rhs = jnp.concatenate([jnp.where(left, xp, 0.0), jnp.where(left, 0.0, xp)], axis=0).astype(BF16)
            y_diag = jnp.dot(jnp.concatenate(mats, axis=1), rhs, preferred_element_type=F32)
            po = slice(pr * LANES, (pr + 1) * LANES)
            y_ref[:, ps] = y_diag + y_off[:, po] + dskip_ref[:, ps] * xp
        st = lax.dot_general(bg, xdec[:, cs], (((0,), (0,)), ((), ())), preferred_element_type=F32)
        state_ref[:, cs] = state_ref[:, cs] * cdec[:, cs] + st

    z = z_ref[...].astype(F32)
    u = y_ref[...] * (z * (1.0 / (1.0 + jnp.exp(-z))))
    for g in range(SSD_GROUPS):
        cs = slice(g * gw, (g + 1) * gw)
        ug = u[:, cs]
        ms = jnp.mean(ug * ug, axis=-1, keepdims=True)
        o_ref[:, cs] = (ug * lax.rsqrt(ms + LN_EPS) * nw_ref[:, cs]).astype(o_ref.dtype)


def _ssd_scan_norm(xbc, z, dt_raw, dtb, alog, dsk, nw, layer, *, layout, name):
    t, d_inner = z.shape
    nheads = d_inner // SSD_HEADDIM
    gn = SSD_GROUPS * D_STATE
    b_col, c_col = d_inner // gn, d_inner // gn + 1
    nchunks = t // CHUNK
    layout_chunks = tuple(v // CHUNK for v in layout)
    expand = np.repeat(np.eye(nheads, dtype=np.float32), SSD_HEADDIM, axis=1)
    e3 = jnp.asarray(np.concatenate([expand] * 3, axis=0), dtype=BF16)
    const = lambda c: (0, 0)
    par = lambda c: (layer, 0, 0)
    rev = lambda c: (nchunks - 1 - c, 0)
    fwd = lambda c: (c, 0)

    prev_b = pl.pallas_call(
        functools.partial(_ssd_bwd_kernel, nchunks=nchunks, layout_chunks=layout_chunks, nheads=nheads),
        grid=(nchunks,),
        in_specs=[pl.BlockSpec((CHUNK, d_inner), rev),
                  pl.BlockSpec((CHUNK, gn), lambda c: (nchunks - 1 - c, b_col)),
                  pl.BlockSpec((CHUNK, 2 * nheads), rev),
                  _resident((None, 1, 2 * nheads), par),
                  _resident((None, 1, 2 * nheads), par),
                  _resident((3 * nheads, d_inner), const)],
        out_specs=pl.BlockSpec((D_STATE, d_inner), rev),
        out_shape=jax.ShapeDtypeStruct((nchunks * D_STATE, d_inner), BF16),
        scratch_shapes=[pltpu.VMEM((D_STATE, d_inner), F32)],
        compiler_params=_cparams("arbitrary"),
        name=name + "_bwd",
    )(xbc, xbc, dt_raw, dtb, alog, e3)

    return pl.pallas_call(
        functools.partial(_ssd_fwd_kernel, layout_chunks=layout_chunks, nheads=nheads),
        grid=(nchunks,),
        in_specs=[pl.BlockSpec((CHUNK, d_inner), fwd),
                  pl.BlockSpec((CHUNK, d_inner), fwd),
                  pl.BlockSpec((CHUNK, gn), lambda c: (c, b_col)),
                  pl.BlockSpec((CHUNK, gn), lambda c: (c, c_col)),
                  pl.BlockSpec((CHUNK, 2 * nheads), fwd),
                  pl.BlockSpec((D_STATE, d_inner), fwd),
                  _resident((None, 1, 2 * nheads), par),
                  _resident((None, 1, 2 * nheads), par),
                  _resident((None, 1, d_inner), par),
                  _resident((None, 1, d_inner), par),
                  _resident((3 * nheads, d_inner), const)],
        out_specs=pl.BlockSpec((CHUNK, d_inner), fwd),
        out_shape=jax.ShapeDtypeStruct((t, d_inner), BF16),
        scratch_shapes=[pltpu.VMEM((D_STATE, d_inner), F32), pltpu.VMEM((CHUNK, d_inner), F32)],
        compiler_params=_cparams("arbitrary"),
        name=name + "_fwd",
    )(xbc, z, xbc, xbc, dt_raw, prev_b, dtb, alog, dsk, nw, e3)


def _sub_major_kernel(x_ref, *refs, rows, emit_natural):
    nd = (len(refs) - emit_natural) // 2
    bm, d = x_ref.shape
    for r0 in range(0, bm, rows):
        blk = x_ref[r0:r0 + rows, :].astype(BF16)
        if emit_natural:
            refs[-1][r0:r0 + rows, :] = blk
        for p_ref, o_ref in zip(refs[:nd], refs[nd:2 * nd]):
            dil = o_ref.shape[0]
            sub = jnp.dot(p_ref[...], blk, preferred_element_type=F32).astype(BF16)
            o_ref[:, r0 // dil:(r0 + rows) // dil, :] = sub.reshape(dil, rows // dil, d)


def _to_sub_major(xb, dils):
    t, d = xb.shape
    emit_natural = xb.dtype != BF16
    bm, rows = TILES["mix_bm"], TILES["out_rows"]
    perms = [jnp.asarray(_unpermute_matrix(rows, dil).T, dtype=BF16) for dil in dils]
    outs = pl.pallas_call(
        functools.partial(_sub_major_kernel, rows=rows, emit_natural=emit_natural),
        grid=(t // bm,),
        in_specs=[pl.BlockSpec((bm, d), lambda i: (i, 0))]
                 + [_resident((rows, rows), lambda i: (0, 0)) for _ in dils],
        out_specs=[pl.BlockSpec((dil, bm // dil, d), lambda i: (0, i, 0)) for dil in dils]
                  + [pl.BlockSpec((bm, d), lambda i: (i, 0))] * emit_natural,
        out_shape=[jax.ShapeDtypeStruct((dil, t // dil, d), BF16) for dil in dils]
                  + [jax.ShapeDtypeStruct((t, d), BF16)] * emit_natural,
        compiler_params=_cparams("parallel"),
        name="sub_major",
    )(xb, *perms)
    return [o.reshape(t, d) for o in outs]


def _alibi_slopes(n):
    return np.float32(2.0) ** (np.float32(-8.0) * np.arange(1, n + 1, dtype=np.float32) / np.float32(n))


def _attention_layer(x, xb, w_qkv, w_o, layer, g, b, *, alpha, layout):
    ngroups = len(DILATED_CONFIGS)
    hd_all = ATTN_HEADS * HEAD_DIM
    slopes = _alibi_slopes(ngroups * ATTN_HEADS).reshape(ngroups, ATTN_HEADS)
    outs, lses = [], []
    assert DILATED_CONFIGS[0][1] == 1
    subs = _to_sub_major(x if xb is None else xb, [dil for _, dil in DILATED_CONFIGS[1:]])
    xgs = [subs[-1] if xb is None else xb] + subs[:ngroups - 1]
    for gi, (window, dil) in enumerate(DILATED_CONFIGS):
        qkv = _matmul(xgs[gi], w_qkv, layer, gi * 3 * hd_all, 3 * hd_all, BF16, name=f"qkv_g{gi}")
        o, lse = _band_attention(qkv, dil=dil, window=window, slopes=slopes[gi], layout=layout, name=f"attn_g{gi}")
        outs.append(o)
        lses.append(lse)
    mix = _mix_groups(outs, lses, name="attn_mix")
    return _matmul_res_ln(mix, w_o, layer, x, g, b, alpha=alpha, name="attn_out", whole_k=True)


def _ssd_layer(x, xb, w_in, conv_w, conv_b, dtb, alog, dsk, nw, w_out, layer, g, b, *, alpha, layout):
    d_inner = w_out.shape[1]
    nheads = d_inner // SSD_HEADDIM
    gn = SSD_GROUPS * D_STATE
    conv_dim = d_inner + 2 * gn
    z = _matmul(xb, w_in, layer, 0, d_inner, BF16, name="ssd_in_z")
    xbc = _matmul(xb, w_in, layer, d_inner, conv_dim, BF16, name="ssd_in_xbc")
    dt_raw = _matmul(xb, w_in, layer, d_inner + conv_dim, 2 * nheads, F32, name="ssd_in_dt")
    xbc = _conv_silu(xbc, conv_w, conv_b, layer, layout=layout, name="ssd_conv")
    y = _ssd_scan_norm(xbc, z, dt_raw, dtb, alog, dsk, nw, layer, layout=layout, name="ssd_scan")
    return _matmul_res_ln(y, w_out, layer, x, g, b, alpha=alpha, name="ssd_out")


def kernel(x_prompt, x_sample, attn_w_qkv, attn_w_o, ssd_w_in, ssd_conv_w, ssd_conv_b, ssd_dt_bias, ssd_a_log,
           ssd_d, ssd_norm_w, ssd_w_out, mlp_w1, mlp_w2, ln_g, ln_b):
    bp, sp, d = x_prompt.shape
    bs, ss, _ = x_sample.shape
    tp, ts = bp * sp, bs * ss
    depth = mlp_w1.shape[0]
    nssd, d_inner = ssd_w_out.shape[0], ssd_w_out.shape[1]
    nheads = d_inner // SSD_HEADDIM
    alpha = (2.0 * depth) ** 0.25
    layout = (tp, sp, ss)
    x = jnp.concatenate([x_prompt.reshape(tp, d), x_sample.reshape(ts, d)], axis=0)
    xb = None
    w_qkv, w_o = attn_w_qkv, attn_w_o.astype(BF16)
    w_in, w_out = ssd_w_in, ssd_w_out.astype(BF16)
    w1, w2 = mlp_w1, mlp_w2.astype(BF16)
    conv_b = ssd_conv_b.reshape(nssd, 1, -1)
    dtb = ssd_dt_bias.reshape(nssd, 1, 2 * nheads).astype(F32)
    alog = ssd_a_log.reshape(nssd, 1, 2 * nheads).astype(F32)
    dsk = jnp.repeat(ssd_d.astype(F32), SSD_HEADDIM, axis=1).reshape(nssd, 1, d_inner)
    nw = ssd_norm_w.reshape(nssd, 1, d_inner).astype(F32)
    for i in range(depth):
        j = i // 2
        g0, b0 = ln_g[i, 0].reshape(1, d), ln_b[i, 0].reshape(1, d)
        g1, b1 = ln_g[i, 1].reshape(1, d), ln_b[i, 1].reshape(1, d)
        if i % 2 == 0:
            x, xb = _attention_layer(x, xb, w_qkv, w_o, j, g0, b0, alpha=alpha, layout=layout)
        else:
            x, xb = _ssd_layer(x, xb, w_in, ssd_conv_w, conv_b, dtb, alog, dsk, nw, w_out, j, g0, b0,
                               alpha=alpha, layout=layout)
        hid = _matmul(xb, w1, i, 0, w1.shape[2], BF16, act="relu2", name="mlp_up")
        x, xb = _matmul_res_ln(hid, w2, i, x, g1, b1, alpha=alpha, name="mlp_down")
    return x[:tp].reshape(bp, sp, d), x[tp:].reshape(bs, ss, d)
```

```python
import functools
import math

import numpy as np
import jax
import jax.numpy as jnp
from jax import lax
from jax.experimental import pallas as pl
from jax.experimental.pallas import tpu as pltpu

F32 = jnp.float32
BF16 = jnp.bfloat16

DILATED_CONFIGS = ((128, 1), (512, 4), (2048, 16))
ATTN_HEADS = 16
HEAD_DIM = 128
SSD_HEADDIM = 64
SSD_GROUPS = 8
D_STATE = 128
D_CONV = 5
CHUNK = 128
LN_EPS = 1e-5
NEG_BIG = -1e30
LOG2E = 1.4426950408889634

V7X_VMEM_LIMIT_BYTES = 56 * 1024 * 1024
LANES = 128
SUBLANES = 8
BF16_ROWS = 16
CONV_ROWS = 128

TILES = dict(
    proj_bm=1024, proj_bn=1024,
    out_bm=1024, out_bk=1024,
    out_rows=256,
    wo_bm=512,
    mix_bm=512,
    attn_tq=512,
    attn_qb=128,
    conv_tr=1024, conv_bc=1024,
)


def _cparams(*sem):
    return pltpu.CompilerParams(dimension_semantics=sem, vmem_limit_bytes=V7X_VMEM_LIMIT_BYTES)


def _resident(block_shape, index_map):
    return pl.BlockSpec(block_shape, index_map, pipeline_mode=pl.Buffered(1))


def _layer_norm_rows(y, g, b):
    mu = jnp.mean(y, axis=-1, keepdims=True)
    yc = y - mu
    var = jnp.mean(yc * yc, axis=-1, keepdims=True)
    return yc * lax.rsqrt(var + LN_EPS) * g + b


def _seq_bounds(row, layout):
    rows_p, seq_p, seq_s = layout
    in_p = row < rows_p
    lo_p = (row // seq_p) * seq_p
    lo_s = rows_p + ((row - rows_p) // seq_s) * seq_s
    lo = jnp.where(in_p, lo_p, lo_s)
    return lo, lo + jnp.where(in_p, seq_p, seq_s)


def _split3(x):
    hi = x.astype(BF16)
    r1 = x - hi.astype(F32)
    mid = r1.astype(BF16)
    lo = (r1 - mid.astype(F32)).astype(BF16)
    return hi, mid, lo


def _mm_kernel(x_ref, w_ref, o_ref, wb_ref, *, act):
    @pl.when(pl.program_id(1) == 0)
    def _():
        wb_ref[...] = w_ref[...].astype(BF16)

    acc = jnp.dot(x_ref[...], wb_ref[...], preferred_element_type=F32)
    if act == "relu2":
        acc = jnp.maximum(acc, 0.0)
        acc = acc * acc
    o_ref[...] = acc.astype(o_ref.dtype)


def _matmul(x, w, layer, col0, n, out_dtype, *, act=None, name):
    t, k = x.shape
    bm = TILES["proj_bm"]
    bn = min(TILES["proj_bn"], n)
    c0 = col0 // bn
    return pl.pallas_call(
        functools.partial(_mm_kernel, act=act),
        grid=(n // bn, t // bm),
        in_specs=[pl.BlockSpec((bm, k), lambda j, i: (i, 0)),
                  pl.BlockSpec((None, k, bn), lambda j, i: (layer, 0, c0 + j))],
        out_specs=pl.BlockSpec((bm, bn), lambda j, i: (i, j)),
        out_shape=jax.ShapeDtypeStruct((t, n), out_dtype),
        scratch_shapes=[pltpu.VMEM((k, bn), BF16)],
        compiler_params=_cparams("parallel", "arbitrary"),
        name=name,
    )(x, w)


def _mm_ln_kernel(l_ref, w_ref, r_ref, g_ref, b_ref, o_ref, ob_ref, *, nk, alpha, rows):
    kk = pl.program_id(1)

    if nk > 1:
        @pl.when(kk == 0)
        def _():
            o_ref[...] = jnp.dot(l_ref[...], w_ref[...], preferred_element_type=F32)

        @pl.when((kk > 0) & (kk < nk - 1))
        def _():
            o_ref[...] += jnp.dot(l_ref[...], w_ref[...], preferred_element_type=F32)

        wn = r_ref.shape[1]
        for c in range(nk):
            @pl.when(kk == c)
            def _(c=c):
                o_ref[:, c * wn:(c + 1) * wn] += alpha * r_ref[...]

    @pl.when(kk == nk - 1)
    def _():
        bm = l_ref.shape[0]
        for r0 in range(0, bm, rows):
            rs = slice(r0, r0 + rows)
            acc = jnp.dot(l_ref[rs, :], w_ref[...], preferred_element_type=F32)
            acc = acc + (o_ref[rs, :] if nk > 1 else alpha * r_ref[rs, :])
            y = _layer_norm_rows(acc, g_ref[...], b_ref[...])
            o_ref[rs, :] = y
            ob_ref[rs, :] = y.astype(BF16)


def _matmul_res_ln(lhs, w, layer, res, g, b, *, alpha, name, whole_k=False):
    t, k = lhs.shape
    n = w.shape[2]
    bm, bk = (TILES["wo_bm"], k) if whole_k else (TILES["out_bm"], TILES["out_bk"])
    nk = k // bk
    w_spec = (_resident if nk == 1 else pl.BlockSpec)((None, bk, n), lambda i, kk: (layer, kk, 0))
    return pl.pallas_call(
        functools.partial(_mm_ln_kernel, nk=nk, alpha=alpha, rows=TILES["out_rows"]),
        grid=(t // bm, nk),
        in_specs=[pl.BlockSpec((bm, bk), lambda i, kk: (i, kk)),
                  w_spec,
                  pl.BlockSpec((bm, n // nk), lambda i, kk: (i, kk)),
                  _resident((1, n), lambda i, kk: (0, 0)),
                  _resident((1, n), lambda i, kk: (0, 0))],
        out_specs=[pl.BlockSpec((bm, n), lambda i, kk: (i, 0)),
                   pl.BlockSpec((bm, n), lambda i, kk: (i, 0))],
        out_shape=[jax.ShapeDtypeStruct((t, n), F32), jax.ShapeDtypeStruct((t, n), BF16)],
        compiler_params=_cparams("parallel", "arbitrary"),
        name=name,
    )(lhs, w, res, g, b)


def _attn_kernel(q_ref, kp_ref, kc_ref, kn_ref, vp_ref, vc_ref, vn_ref, bias_ref, o_ref, lse_ref,
                 k_scr, v_scr, *, tq, half, layout):
    j = pl.program_id(1)
    row0 = j * tq
    wk = tq + 2 * half
    k_scr[0:half, :] = kp_ref[...]
    k_scr[half:half + tq, :] = kc_ref[...]
    k_scr[half + tq:wk, :] = kn_ref[...]
    v_scr[0:half, :] = vp_ref[...]
    v_scr[half:half + tq, :] = vc_ref[...]
    v_scr[half + tq:wk, :] = vn_ref[...]

    scale = HEAD_DIM ** -0.5
    qb = bias_ref.shape[1]
    wkb = qb + 2 * half
    lane = lax.broadcasted_iota(jnp.int32, (qb, LANES), 1)

    def attend(near_sequence_end):
        for q0 in range(0, tq, qb):
            qs, ks = slice(q0, q0 + qb), slice(q0, q0 + wkb)
            if near_sequence_end:
                lo, hi = _seq_bounds(row0 + q0, layout)
                kpos = row0 + q0 - half + lax.broadcasted_iota(jnp.int32, (1, wkb), 1)
                edge = jnp.where((kpos >= lo) & (kpos < hi), 0.0, NEG_BIG).astype(F32)
            lse_all = jnp.zeros((qb, LANES), F32)
            for h in range(ATTN_HEADS):
                cs = slice(h * HEAD_DIM, (h + 1) * HEAD_DIM)
                z = lax.dot_general(q_ref[qs, cs], k_scr[ks, cs], (((1,), (1,)), ((), ())),
                                    preferred_element_type=F32)
                z = z + bias_ref[h]
                if near_sequence_end:
                    z = z + edge
                m = jnp.max(z, axis=-1, keepdims=True)
                p = jnp.exp2((z - m) * (scale * LOG2E))
                den = jnp.sum(p, axis=-1, keepdims=True)
                pv = jnp.dot(p.astype(BF16), v_scr[ks, cs], preferred_element_type=F32)
                o_ref[qs, cs] = (pv * (1.0 / den)).astype(o_ref.dtype)
                lse_all = jnp.where(lane == h, scale * m + jnp.log(den), lse_all)
            lse_ref[qs, :] = lse_all

    lo0, hi0 = _seq_bounds(row0, layout)
    interior = (row0 > lo0) & (row0 + tq < hi0)
    pl.when(interior)(functools.partial(attend, False))
    pl.when(jnp.logical_not(interior))(functools.partial(attend, True))


def _band_bias(qb, half, dil, slopes):
    wk = qb + 2 * half
    rel = np.arange(wk)[None, :] - half - np.arange(qb)[:, None]
    dist = (np.abs(rel) * dil).astype(np.float32)
    scale = np.float32(HEAD_DIM ** -0.5)
    bias = -(slopes.astype(np.float32)[:, None, None] * dist[None]) / scale
    return np.where((np.abs(rel) <= half)[None], bias, np.float32(NEG_BIG)).astype(np.float32)


def _band_attention(qkv, *, dil, window, slopes, layout, name):
    t = qkv.shape[0]
    hd_all = ATTN_HEADS * HEAD_DIM
    half = window // (2 * dil)
    rows = t // dil
    qb = TILES["attn_qb"]
    assert all(v % (dil * qb) == 0 for v in layout), "a query sub-block must lie inside one sequence"
    tq = next(c for c in range(TILES["attn_tq"], 0, -qb) if rows % c == 0)
    wk = tq + 2 * half
    nt = rows // tq
    per = tq // half
    nhalf = t // half
    lay = tuple(v // dil for v in layout)
    bias = jnp.asarray(_band_bias(TILES["attn_qb"], half, dil, slopes))

    def cur(col):
        return pl.BlockSpec((tq, hd_all), lambda r, j: (r * nt + j, col))

    def prev(col):
        return pl.BlockSpec((half, hd_all), lambda r, j: (jnp.maximum((r * nt + j) * per - 1, 0), col))

    def nxt(col):
        return pl.BlockSpec((half, hd_all), lambda r, j: (jnp.minimum((r * nt + j + 1) * per, nhalf - 1), col))

    return pl.pallas_call(
        functools.partial(_attn_kernel, tq=tq, half=half, layout=lay),
        grid=(dil, nt),
        in_specs=[cur(0), prev(1), cur(1), nxt(1), prev(2), cur(2), nxt(2),
                  _resident(bias.shape, lambda r, j: (0, 0, 0))],
        out_specs=[pl.BlockSpec((tq, hd_all), lambda r, j: (r * nt + j, 0)),
                   pl.BlockSpec((tq, LANES), lambda r, j: (r * nt + j, 0))],
        out_shape=[jax.ShapeDtypeStruct((t, hd_all), BF16), jax.ShapeDtypeStruct((t, LANES), F32)],
        scratch_shapes=[pltpu.VMEM((wk, hd_all), BF16), pltpu.VMEM((wk, hd_all), BF16)],
        compiler_params=_cparams("parallel", "parallel"),
        name=name,
    )(qkv, qkv, qkv, qkv, qkv, qkv, qkv, bias)


def _mix_kernel(o0_ref, o1_ref, o2_ref, l0_ref, l1_ref, l2_ref, p1_ref, p2_ref, ex_ref, mix_ref, *, rows):
    bm, k = o0_ref.shape

    def unpermute(perm_ref, o_blk_ref, l_blk_ref, r0):
        dil = o_blk_ref.shape[0]
        run = slice(r0 // dil, (r0 + rows) // dil)
        o_nat = jnp.dot(perm_ref[...], o_blk_ref[:, run, :].reshape(rows, k), preferred_element_type=F32)
        l3 = jnp.dot(perm_ref[...], jnp.concatenate(_split3(l_blk_ref[:, run, :].reshape(rows, LANES)), axis=1),
                     preferred_element_type=F32)
        return o_nat, l3[:, 0:LANES] + l3[:, LANES:2 * LANES] + l3[:, 2 * LANES:3 * LANES]

    for r0 in range(0, bm, rows):
        rs = slice(r0, r0 + rows)
        o1, l1 = unpermute(p1_ref, o1_ref, l1_ref, r0)
        o2, l2 = unpermute(p2_ref, o2_ref, l2_ref, r0)
        l0 = l0_ref[rs, :]
        m = jnp.maximum(jnp.maximum(l0, l1), l2)
        e0, e1, e2 = jnp.exp(l0 - m), jnp.exp(l1 - m), jnp.exp(l2 - m)
        inv = 1.0 / (e0 + e1 + e2)

        def over_head_lanes(w):
            hi = w.astype(BF16)
            lo = (w - hi.astype(F32)).astype(BF16)
            return jnp.dot(jnp.concatenate([hi, lo], axis=1), ex_ref[...], preferred_element_type=F32)

        o0 = o0_ref[rs, :].astype(F32)
        mix = o0 + over_head_lanes(e1 * inv) * (o1 - o0) + over_head_lanes(e2 * inv) * (o2 - o0)
        mix_ref[rs, :] = mix.astype(BF16)


def _unpermute_matrix(bm, dil):
    p = np.zeros((bm, bm), np.float32)
    sub = np.arange(bm)
    r, i = sub // (bm // dil), sub % (bm // dil)
    p[i * dil + r, sub] = 1.0
    return p


def _mix_groups(outs, lses, *, name):
    t, k = outs[0].shape
    bm = TILES["mix_bm"]
    dils = [dil for _, dil in DILATED_CONFIGS]
    row = lambda i: (i, 0)
    const = lambda i: (0, 0)

    def sub_major(a, dil):
        return a.reshape(dil, t // dil, a.shape[1])

    def sub_spec(cols, dil):
        return pl.BlockSpec((dil, bm // dil, cols), lambda i: (0, i, 0))

    rows = TILES["out_rows"]
    perms = [jnp.asarray(_unpermute_matrix(rows, dil), dtype=BF16) for dil in dils[1:]]
    head_rows = np.zeros((LANES, k), np.float32)
    head_rows[:ATTN_HEADS] = np.repeat(np.eye(ATTN_HEADS, dtype=np.float32), HEAD_DIM, axis=1)
    expand = jnp.asarray(np.concatenate([head_rows, head_rows], axis=0), dtype=BF16)
    return pl.pallas_call(
        functools.partial(_mix_kernel, rows=rows),
        grid=(t // bm,),
        in_specs=[pl.BlockSpec((bm, k), row), sub_spec(k, dils[1]), sub_spec(k, dils[2]),
                  pl.BlockSpec((bm, LANES), row), sub_spec(LANES, dils[1]), sub_spec(LANES, dils[2]),
                  _resident((rows, rows), const), _resident((rows, rows), const),
                  _resident((2 * LANES, k), const)],
        out_specs=pl.BlockSpec((bm, k), row),
        out_shape=jax.ShapeDtypeStruct((t, k), BF16),
        compiler_params=_cparams("parallel"),
        name=name,
    )(outs[0], sub_major(outs[1], dils[1]), sub_major(outs[2], dils[2]),
      lses[0], sub_major(lses[1], dils[1]), sub_major(lses[2], dils[2]),
      perms[0], perms[1], expand)


def _conv_kernel(p_ref, c_ref, n_ref, s_ref, w_ref, b_ref, o_ref, ext_ref, *, tr, layout):
    i = pl.program_id(0)
    row0 = i * tr
    lo, hi = _seq_bounds(row0, layout)
    halo = BF16_ROWS
    zero = jnp.zeros_like(p_ref)
    ext_ref[0:halo, :] = jnp.where(row0 == lo, zero, p_ref[...])
    ext_ref[halo:halo + tr, :] = c_ref[...]
    ext_ref[halo + tr:tr + 2 * halo, :] = jnp.where(row0 + tr == hi, zero, n_ref[...])
    rb = CONV_ROWS
    mid = D_CONV // 2
    taps = [kk for kk in range(D_CONV) if kk != mid]
    for r0 in range(0, tr, rb):
        blk = ext_ref[r0:r0 + rb + 2 * halo, :]
        shifted = jnp.dot(s_ref[...], blk, preferred_element_type=F32)
        acc = b_ref[...] + w_ref[mid:mid + 1, :] * blk[halo:halo + rb, :].astype(F32)
        for j, kk in enumerate(taps):
            acc = acc + w_ref[kk:kk + 1, :] * shifted[j * rb:(j + 1) * rb, :]
        o_ref[r0:r0 + rb, :] = (acc * (1.0 / (1.0 + jnp.exp2(acc * (-LOG2E))))).astype(o_ref.dtype)


def _shift_matrix(rb, halo):
    offs = [kk - D_CONV // 2 for kk in range(D_CONV) if kk != D_CONV // 2]
    s = np.zeros((len(offs) * rb, rb + 2 * halo), np.float32)
    for j, off in enumerate(offs):
        s[j * rb + np.arange(rb), halo + np.arange(rb) + off] = 1.0
    return s


def _conv_silu(u, w, b, layer, *, layout, name):
    t, c = u.shape
    tr, bc = TILES["conv_tr"], TILES["conv_bc"]
    per = tr // BF16_ROWS
    nh = t // BF16_ROWS
    shift = jnp.asarray(_shift_matrix(CONV_ROWS, BF16_ROWS), dtype=BF16)
    return pl.pallas_call(
        functools.partial(_conv_kernel, tr=tr, layout=layout),
        grid=(t // tr, c // bc),
        in_specs=[pl.BlockSpec((BF16_ROWS, bc), lambda i, j: (jnp.maximum(i * per - 1, 0), j)),
                  pl.BlockSpec((tr, bc), lambda i, j: (i, j)),
                  pl.BlockSpec((BF16_ROWS, bc), lambda i, j: (jnp.minimum((i + 1) * per, nh - 1), j)),
                  _resident(shift.shape, lambda i, j: (0, 0)),
                  pl.BlockSpec((None, D_CONV, bc), lambda i, j: (layer, 0, j)),
                  pl.BlockSpec((None, 1, bc), lambda i, j: (layer, 0, j))],
        out_specs=pl.BlockSpec((tr, bc), lambda i, j: (i, j)),
        out_shape=jax.ShapeDtypeStruct((t, c), BF16),
        scratch_shapes=[pltpu.VMEM((tr + 2 * BF16_ROWS, bc), BF16)],
        compiler_params=_cparams("parallel", "parallel"),
        name=name,
    )(u, u, u, shift, w, b)


def _cumsum_rows(tri, la):
    out = jnp.dot(tri, jnp.concatenate(_split3(la), axis=1), preferred_element_type=F32)
    return out[:, 0:LANES] + out[:, LANES:2 * LANES] + out[:, 2 * LANES:3 * LANES]


def _expand_heads(q, e3_ref):
    return jnp.dot(jnp.concatenate(_split3(q), axis=1), e3_ref[...], preferred_element_type=F32)


def _softplus(x):
    return jnp.maximum(x, 0.0) + jnp.log(1.0 + jnp.exp(-jnp.abs(x)))


def _chunk_flags(c, layout_chunks):
    lo, hi = _seq_bounds(c, layout_chunks)
    return c == lo, c == hi - 1


def _ssd_bwd_kernel(x_ref, b_ref, dt_ref, dtb_ref, alog_ref, e3_ref, prev_ref, state_ref,
                    *, nchunks, layout_chunks, nheads):
    c = nchunks - 1 - pl.program_id(0)
    _, is_last = _chunk_flags(c, layout_chunks)

    @pl.when(is_last)
    def _():
        state_ref[...] = jnp.zeros_like(state_ref)

    prev_ref[...] = state_ref[...].astype(BF16)

    dt = _softplus(dt_ref[...] + dtb_ref[...])
    la = dt * (-jnp.exp(alog_ref[...]))
    ti = lax.broadcasted_iota(jnp.int32, (CHUNK, CHUNK), 0)
    si = lax.broadcasted_iota(jnp.int32, (CHUNK, CHUNK), 1)
    triu = jnp.where(si >= ti, 1.0, 0.0).astype(BF16)
    acs = _cumsum_rows(triu, la)[:, nheads:2 * nheads]
    dt_b = dt[:, nheads:2 * nheads]
    wgt = dt_b * jnp.exp(acs[0:1, :] - acs)
    xdec = (x_ref[...] * _expand_heads(wgt, e3_ref)).astype(BF16)
    cdec = _expand_heads(jnp.broadcast_to(jnp.exp(acs[0:1, :]), (SUBLANES, nheads)), e3_ref)[0:1, :]
    gw = (nheads // SSD_GROUPS) * SSD_HEADDIM
    for g in range(SSD_GROUPS):
        cs = slice(g * gw, (g + 1) * gw)
        bg = b_ref[:, g * D_STATE:(g + 1) * D_STATE]
        st = lax.dot_general(bg, xdec[:, cs], (((0,), (0,)), ((), ())), preferred_element_type=F32)
        state_ref[:, cs] = state_ref[:, cs] * cdec[:, cs] + st


def _ssd_fwd_kernel(x_ref, z_ref, b_ref, c_ref, dt_ref, prevb_ref, dtb_ref, alog_ref, dskip_ref,
                    nw_ref, e3_ref, o_ref, state_ref, y_ref, *, layout_chunks, nheads):
    c = pl.program_id(0)
    is_first, _ = _chunk_flags(c, layout_chunks)

    @pl.when(is_first)
    def _():
        state_ref[...] = jnp.zeros_like(state_ref)

    dt = _softplus(dt_ref[...] + dtb_ref[...])
    la = dt * (-jnp.exp(alog_ref[...]))
    ti = lax.broadcasted_iota(jnp.int32, (CHUNK, CHUNK), 0)
    si = lax.broadcasted_iota(jnp.int32, (CHUNK, CHUNK), 1)
    lower = ti >= si
    upper = si >= ti
    lanes2h = lax.broadcasted_iota(jnp.int32, (CHUNK, 2 * nheads), 1)
    acs = jnp.where(lanes2h < nheads,
                    _cumsum_rows(jnp.where(lower, 1.0, 0.0).astype(BF16), la),
                    _cumsum_rows(jnp.where(upper, 1.0, 0.0).astype(BF16), la))
    col2 = acs * LOG2E
    row2_t = (jnp.log(dt) * LOG2E - col2).T
    acs_f, acs_b = acs[:, 0:nheads], acs[:, nheads:2 * nheads]
    e_f = _expand_heads(jnp.exp(acs_f), e3_ref)
    e_b = _expand_heads(jnp.exp(acs_b), e3_ref)
    w_f = dt[:, 0:nheads] * jnp.exp(acs_f[CHUNK - 1:CHUNK, :] - acs_f)
    x = x_ref[...].astype(F32)
    xdec = (x * _expand_heads(w_f, e3_ref)).astype(BF16)
    cdec = e_f[CHUNK - 1:CHUNK, :]

    hpg = nheads // SSD_GROUPS
    gw = hpg * SSD_HEADDIM
    lane = lax.broadcasted_iota(jnp.int32, (CHUNK, LANES), 1)
    left = (lane < SSD_HEADDIM)
    for g in range(SSD_GROUPS):
        cs = slice(g * gw, (g + 1) * gw)
        bg = b_ref[:, g * D_STATE:(g + 1) * D_STATE]
        cg = c_ref[:, g * D_STATE:(g + 1) * D_STATE]
        cb = lax.dot_general(cg, bg, (((1,), (1,)), ((), ())), preferred_element_type=F32)
        y_off = (jnp.dot(cg, state_ref[:, cs].astype(BF16), preferred_element_type=F32) * e_f[:, cs]
                 + jnp.dot(cg, prevb_ref[:, cs], preferred_element_type=F32) * e_b[:, cs])
        for pr in range(hpg // 2):
            mats = []
            for h in (g * hpg + 2 * pr, g * hpg + 2 * pr + 1):
                hb = nheads + h
                seg_f = col2[:, h:h + 1] + row2_t[h:h + 1, :]
                seg_b = col2[:, hb:hb + 1] + row2_t[hb:hb + 1, :]
                dec = jnp.exp2(jnp.where(lower, seg_f, NEG_BIG)) + jnp.exp2(jnp.where(upper, seg_b, NEG_BIG))
                mats.append((cb * dec).astype(BF16))
            ps = slice(g * gw + pr * LANES, g * gw + (pr + 1) * LANES)
            xp = x[:, ps]
            rhs = jnp.concatenate([jnp.where(left, xp, 0.0), jnp.where(left, 0.0, xp)], axis=0).astype(BF16)
            y_diag = jnp.dot(jnp.concatenate(mats, axis=1), rhs, preferred_element_type=F32)
            po = slice(pr * LANES, (pr + 1) * LANES)
            y_ref[:, ps] = y_diag + y_off[:, po] + dskip_ref[:, ps] * xp
        st = lax.dot_general(bg, xdec[:, cs], (((0,), (0,)), ((), ())), preferred_element_type=F32)
        state_ref[:, cs] = state_ref[:, cs] * cdec[:, cs] + st

    z = z_ref[...].astype(F32)
    u = y_ref[...] * (z * (1.0 / (1.0 + jnp.exp(-z))))
    for g in range(SSD_GROUPS):
        cs = slice(g * gw, (g + 1) * gw)
        ug = u[:, cs]
        ms = jnp.mean(ug * ug, axis=-1, keepdims=True)
        o_ref[:, cs] = (ug * lax.rsqrt(ms + LN_EPS) * nw_ref[:, cs]).astype(o_ref.dtype)


def _ssd_scan_norm(xbc, z, dt_raw, dtb, alog, dsk, nw, layer, *, layout, name):
    t, d_inner = z.shape
    nheads = d_inner // SSD_HEADDIM
    gn = SSD_GROUPS * D_STATE
    b_col, c_col = d_inner // gn, d_inner // gn + 1
    nchunks = t // CHUNK
    layout_chunks = tuple(v // CHUNK for v in layout)
    expand = np.repeat(np.eye(nheads, dtype=np.float32), SSD_HEADDIM, axis=1)
    e3 = jnp.asarray(np.concatenate([expand] * 3, axis=0), dtype=BF16)
    const = lambda c: (0, 0)
    par = lambda c: (layer, 0, 0)
    rev = lambda c: (nchunks - 1 - c, 0)
    fwd = lambda c: (c, 0)

    prev_b = pl.pallas_call(
        functools.partial(_ssd_bwd_kernel, nchunks=nchunks, layout_chunks=layout_chunks, nheads=nheads),
        grid=(nchunks,),
        in_specs=[pl.BlockSpec((CHUNK, d_inner), rev),
                  pl.BlockSpec((CHUNK, gn), lambda c: (nchunks - 1 - c, b_col)),
                  pl.BlockSpec((CHUNK, 2 * nheads), rev),
                  _resident((None, 1, 2 * nheads), par),
                  _resident((None, 1, 2 * nheads), par),
                  _resident((3 * nheads, d_inner), const)],
        out_specs=pl.BlockSpec((D_STATE, d_inner), rev),
        out_shape=jax.ShapeDtypeStruct((nchunks * D_STATE, d_inner), BF16),
        scratch_shapes=[pltpu.VMEM((D_STATE, d_inner), F32)],
        compiler_params=_cparams("arbitrary"),
        name=name + "_bwd",
    )(xbc, xbc, dt_raw, dtb, alog, e3)

    return pl.pallas_call(
        functools.partial(_ssd_fwd_kernel, layout_chunks=layout_chunks, nheads=nheads),
        grid=(nchunks,),
        in_specs=[pl.BlockSpec((CHUNK, d_inner), fwd),
                  pl.BlockSpec((CHUNK, d_inner), fwd),
                  pl.BlockSpec((CHUNK, gn), lambda c: (c, b_col)),
                  pl.BlockSpec((CHUNK, gn), lambda c: (c, c_col)),
                  pl.BlockSpec((CHUNK, 2 * nheads), fwd),
                  pl.BlockSpec((D_STATE, d_inner), fwd),
                  _resident((None, 1, 2 * nheads), par),
                  _resident((None, 1, 2 * nheads), par),
                  _resident((None, 1, d_inner), par),
                  _resident((None, 1, d_inner), par),
                  _resident((3 * nheads, d_inner), const)],
        out_specs=pl.BlockSpec((CHUNK, d_inner), fwd),
        out_shape=jax.ShapeDtypeStruct((t, d_inner), BF16),
        scratch_shapes=[pltpu.VMEM((D_STATE, d_inner), F32), pltpu.VMEM((CHUNK, d_inner), F32)],
        compiler_params=_cparams("arbitrary"),
        name=name + "_fwd",
    )(xbc, z, xbc, xbc, dt_raw, prev_b, dtb, alog, dsk, nw, e3)


def _sub_major_kernel(x_ref, *refs, rows, emit_natural):
    nd = (len(refs) - emit_natural) // 2
    bm, d = x_ref.shape
    for r0 in range(0, bm, rows):
        blk = x_ref[r0:r0 + rows, :].astype(BF16)
        if emit_natural:
            refs[-1][r0:r0 + rows, :] = blk
        for p_ref, o_ref in zip(refs[:nd], refs[nd:2 * nd]):
            dil = o_ref.shape[0]
            sub = jnp.dot(p_ref[...], blk, preferred_element_type=F32).astype(BF16)
            o_ref[:, r0 // dil:(r0 + rows) // dil, :] = sub.reshape(dil, rows // dil, d)


def _to_sub_major(xb, dils):
    t, d = xb.shape
    emit_natural = xb.dtype != BF16
    bm, rows = TILES["mix_bm"], TILES["out_rows"]
    perms = [jnp.asarray(_unpermute_matrix(rows, dil).T, dtype=BF16) for dil in dils]
    outs = pl.pallas_call(
        functools.partial(_sub_major_kernel, rows=rows, emit_natural=emit_natural),
        grid=(t // bm,),
        in_specs=[pl.BlockSpec((bm, d), lambda i: (i, 0))]
                 + [_resident((rows, rows), lambda i: (0, 0)) for _ in dils],
        out_specs=[pl.BlockSpec((dil, bm // dil, d), lambda i: (0, i, 0)) for dil in dils]
                  + [pl.BlockSpec((bm, d), lambda i: (i, 0))] * emit_natural,
        out_shape=[jax.ShapeDtypeStruct((dil, t // dil, d), BF16) for dil in dils]
                  + [jax.ShapeDtypeStruct((t, d), BF16)] * emit_natural,
        compiler_params=_cparams("parallel"),
        name="sub_major",
    )(xb, *perms)
    return [o.reshape(t, d) for o in outs]


def _alibi_slopes(n):
    return np.float32(2.0) ** (np.float32(-8.0) * np.arange(1, n + 1, dtype=np.float32) / np.float32(n))


def _attention_layer(x, xb, w_qkv, w_o, layer, g, b, *, alpha, layout):
    ngroups = len(DILATED_CONFIGS)
    hd_all = ATTN_HEADS * HEAD_DIM
    slopes = _alibi_slopes(ngroups * ATTN_HEADS).reshape(ngroups, ATTN_HEADS)
    outs, lses = [], []
    assert DILATED_CONFIGS[0][1] == 1
    subs = _to_sub_major(x if xb is None else xb, [dil for _, dil in DILATED_CONFIGS[1:]])
    xgs = [subs[-1] if xb is None else xb] + subs[:ngroups - 1]
    for gi, (window, dil) in enumerate(DILATED_CONFIGS):
        qkv = _matmul(xgs[gi], w_qkv, layer, gi * 3 * hd_all, 3 * hd_all, BF16, name=f"qkv_g{gi}")
        o, lse = _band_attention(qkv, dil=dil, window=window, slopes=slopes[gi], layout=layout, name=f"attn_g{gi}")
        outs.append(o)
        lses.append(lse)
    mix = _mix_groups(outs, lses, name="attn_mix")
    return _matmul_res_ln(mix, w_o, layer, x, g, b, alpha=alpha, name="attn_out", whole_k=True)


def _ssd_layer(x, xb, w_in, conv_w, conv_b, dtb, alog, dsk, nw, w_out, layer, g, b, *, alpha, layout):
    d_inner = w_out.shape[1]
    nheads = d_inner // SSD_HEADDIM
    gn = SSD_GROUPS * D_STATE
    conv_dim = d_inner + 2 * gn
    z = _matmul(xb, w_in, layer, 0, d_inner, BF16, name="ssd_in_z")
    xbc = _matmul(xb, w_in, layer, d_inner, conv_dim, BF16, name="ssd_in_xbc")
    dt_raw = _matmul(xb, w_in, layer, d_inner + conv_dim, 2 * nheads, F32, name="ssd_in_dt")
    xbc = _conv_silu(xbc, conv_w, conv_b, layer, layout=layout, name="ssd_conv")
    y = _ssd_scan_norm(xbc, z, dt_raw, dtb, alog, dsk, nw, layer, layout=layout, name="ssd_scan")
    return _matmul_res_ln(y, w_out, layer, x, g, b, alpha=alpha, name="ssd_out")


def kernel(x_prompt, x_sample, attn_w_qkv, attn_w_o, ssd_w_in, ssd_conv_w, ssd_conv_b, ssd_dt_bias, ssd_a_log,
           ssd_d, ssd_norm_w, ssd_w_out, mlp_w1, mlp_w2, ln_g, ln_b):
    bp, sp, d = x_prompt.shape
    bs, ss, _ = x_sample.shape
    tp, ts = bp * sp, bs * ss
    depth = mlp_w1.shape[0]
    nssd, d_inner = ssd_w_out.shape[0], ssd_w_out.shape[1]
    nheads = d_inner // SSD_HEADDIM
    alpha = (2.0 * depth) ** 0.25
    layout = (tp, sp, ss)
    x = jnp.concatenate([x_prompt.reshape(tp, d), x_sample.reshape(ts, d)], axis=0)
    xb = None
    w_qkv, w_o = attn_w_qkv, attn_w_o.astype(BF16)
    w_in, w_out = ssd_w_in, ssd_w_out.astype(BF16)
    w1, w2 = mlp_w1, mlp_w2.astype(BF16)
    conv_b = ssd_conv_b.reshape(nssd, 1, -1)
    dtb = ssd_dt_bias.reshape(nssd, 1, 2 * nheads).astype(F32)
    alog = ssd_a_log.reshape(nssd, 1, 2 * nheads).astype(F32)
    dsk = jnp.repeat(ssd_d.astype(F32), SSD_HEADDIM, axis=1).reshape(nssd, 1, d_inner)
    nw = ssd_norm_w.reshape(nssd, 1, d_inner).astype(F32)
    for i in range(depth):
        j = i // 2
        g0, b0 = ln_g[i, 0].reshape(1, d), ln_b[i, 0].reshape(1, d)
        g1, b1 = ln_g[i, 1].reshape(1, d), ln_b[i, 1].reshape(1, d)
        if i % 2 == 0:
            x, xb = _attention_layer(x, xb, w_qkv, w_o, j, g0, b0, alpha=alpha, layout=layout)
        else:
            x, xb = _ssd_layer(x, xb, w_in, ssd_conv_w, conv_b, dtb, alog, dsk, nw, w_out, j, g0, b0,
                               alpha=alpha, layout=layout)
        hid = _matmul(xb, w1, i, 0, w1.shape[2], BF16, act="relu2", name="mlp_up")
        x, xb = _matmul_res_ln(hid, w2, i, x, g1, b1, alpha=alpha, name="mlp_down")
    return x[:tp].reshape(bp, sp, d), x[tp:].reshape(bs, ss, d)
```
